```python
import jax, jax.numpy as jnp
from jax import lax
import numpy as np

D_MODEL = 1024
BATCH = 1
SEQ = 16384
DEPTH = 2

GRID_W = 64
CTX_LEN = 256
N_MIXERS = 4
MIX_W = D_MODEL
GROUP_W = MIX_W // N_MIXERS
EPS = 1e-6

POOL_WINDOWS = (2, 4, 8, 16)
POOL_CH = GROUP_W // len(POOL_WINDOWS)

MLA_HEADS = 4
MLA_NOPE = 64
MLA_ROPE = 32
MLA_QK = MLA_NOPE + MLA_ROPE
MLA_V = GROUP_W // MLA_HEADS
MLA_Q_RANK = 192
MLA_KV_RANK = 128
ROPE_BASE = 10000.0
Q_BLOCK = 128

SGU_HEADS = 4
SGU_HD = GROUP_W // SGU_HEADS
SGU_CHUNK = 128

LRU_HEADS = 4
LRU_HD = GROUP_W // LRU_HEADS
CONV_W = 4
CONV_LEFT = 1
LRU_C = 8.0

N_EXPERTS = 32
TOP_K = 4
D_EXPERT = D_MODEL
SWIGLU_LIMIT = 7.0
SWIGLU_ALPHA = 1.702
MOE_BLOCK = 256

COL_SIZES = (GROUP_W, MLA_Q_RANK, MLA_KV_RANK, MLA_ROPE, 2 * GROUP_W, GROUP_W, GROUP_W)
IN_COLS = sum(COL_SIZES)

kernel_name = 'hybrid_dit_parallel_groups'


def rms_norm(x, g):
    xf = x.astype(jnp.float32)
    y = xf * lax.rsqrt(jnp.mean(xf * xf, axis=-1, keepdims=True) + EPS)
    return (y * g.astype(jnp.float32)).astype(x.dtype)


def layer_norm(x, g, b):
    xf = x.astype(jnp.float32)
    mu = jnp.mean(xf, axis=-1, keepdims=True)
    var = jnp.mean(jnp.square(xf - mu), axis=-1, keepdims=True)
    y = (xf - mu) * lax.rsqrt(var + EPS)
    return (y * g.astype(jnp.float32) + b.astype(jnp.float32)).astype(x.dtype)


def modulate(h, shift, scale):
    return h * (1.0 + scale) + shift


def split_cols(p):
    idx, s = [], 0
    for z in COL_SIZES[:-1]:
        s += z
        idx.append(s)
    return jnp.split(p, idx, axis=-1)


def axial_rope_tables(rows):
    n_freq = MLA_ROPE // 4
    inv = ROPE_BASE ** (-jnp.arange(n_freq, dtype=jnp.float32) / n_freq)
    r = jnp.repeat(jnp.arange(rows, dtype=jnp.float32), GRID_W)
    col = jnp.tile(jnp.arange(GRID_W, dtype=jnp.float32), rows)
    ang = jnp.concatenate([r[:, None] * inv, col[:, None] * inv], axis=-1)
    return jnp.cos(ang), jnp.sin(ang)


def rope_tail(x, cos, sin):
    xn, xr = x[..., :MLA_NOPE], x[..., MLA_NOPE:]
    x1, x2 = jnp.split(xr, 2, axis=-1)
    cs = cos[None, :, None, :].astype(x.dtype)
    sn = sin[None, :, None, :].astype(x.dtype)
    return jnp.concatenate([xn, x1 * cs - x2 * sn, x1 * sn + x2 * cs], axis=-1)


def multiscale_pool(x, pool_w, pool_scale):
    B, N, _ = x.shape
    xf = x.astype(jnp.float32)
    csum = jnp.concatenate([jnp.zeros((B, 1, GROUP_W), jnp.float32), jnp.cumsum(xf, axis=1)], axis=1)
    t = jnp.arange(N)
    outs = []
    for g, w in enumerate(POOL_WINDOWS):
        lo = jnp.clip(t - w // 2, 0, N)
        hi = jnp.clip(t + w // 2, 0, N)
        cg = csum[..., g * POOL_CH:(g + 1) * POOL_CH]
        mean = (cg[:, hi] - cg[:, lo]) / (hi - lo).astype(jnp.float32)[:, None]
        diff = (mean - xf[..., g * POOL_CH:(g + 1) * POOL_CH]).astype(x.dtype)
        outs.append(diff @ pool_w[g])
    return jnp.concatenate(outs, axis=-1) * pool_scale


def mla_q(qa, q_a_norm, w_uq, q_norm):
    B, N, _ = qa.shape
    q = (rms_norm(qa, q_a_norm) @ w_uq).reshape(B, N, MLA_HEADS, MLA_QK)
    return rms_norm(q, q_norm)


def mla_kv(kva, kr, kv_a_norm, w_ukv, k_norm):
    B, N, _ = kva.shape
    kv = (rms_norm(kva, kv_a_norm) @ w_ukv).reshape(B, N, MLA_HEADS, MLA_NOPE + MLA_V)
    k_rope = jnp.broadcast_to(kr[:, :, None, :], (B, N, MLA_HEADS, MLA_ROPE))
    k = jnp.concatenate([kv[..., :MLA_NOPE], k_rope], axis=-1)
    return rms_norm(k, k_norm), kv[..., MLA_NOPE:]


def block_attention(q, k, v):
    B, Nq, H, dq = q.shape
    nb = Nq // Q_BLOCK
    qb = q.reshape(B, nb, Q_BLOCK, H, dq).transpose(1, 0, 2, 3, 4)
    scale = dq ** -0.5

    def one_block(qblk):
        s = jnp.einsum('bqhd,bkhd->bhqk', qblk, k).astype(jnp.float32) * scale
        p = jax.nn.softmax(s, axis=-1).astype(v.dtype)
        return jnp.einsum('bhqk,bkhd->bqhd', p, v)

    o = lax.map(one_block, qb)
    return o.transpose(1, 0, 2, 3, 4).reshape(B, Nq, H * v.shape[-1])


def chunk_sgu(z, norm_g, norm_b, ws, bs):
    B, N, _ = z.shape
    z = jax.nn.gelu(z)
    u, v = jnp.split(z, 2, axis=-1)
    v = layer_norm(v, norm_g, norm_b)
    nc = N // SGU_CHUNK
    v = v.reshape(B, nc, SGU_CHUNK, SGU_HEADS, SGU_HD)
    v = jnp.einsum('hpq,bcqhd->bcphd', ws, v) + bs.T[None, None, :, :, None]
    return u * v.reshape(B, N, GROUP_W)


def centred_dwconv(x, w, b):
    n = x.shape[1]
    xp = jnp.pad(x, ((0, 0), (CONV_LEFT, CONV_W - 1 - CONV_LEFT), (0, 0)))
    out = b
    for k in range(CONV_W):
        out = out + xp[:, k:k + n] * w[k]
    return out


def rglru_coeffs(x, wa, ba, wx, bx, lam):
    B, N, _ = x.shape
    xf = x.astype(jnp.float32)
    xh = xf.reshape(B, N, LRU_HEADS, LRU_HD)
    r = jax.nn.sigmoid(jnp.einsum('bnhi,hij->bnhj', xh, wa.astype(jnp.float32)).reshape(B, N, GROUP_W) + ba)
    i = jax.nn.sigmoid(jnp.einsum('bnhi,hij->bnhj', xh, wx.astype(jnp.float32)).reshape(B, N, GROUP_W) + bx)
    log_a = -LRU_C * r * jax.nn.softplus(-lam.astype(jnp.float32))
    a = jnp.exp(log_a)
    drive = jnp.sqrt(-jnp.expm1(2.0 * log_a)) * (i * xf)
    return a, drive


def _combine(e1, e2):
    a1, b1 = e1
    a2, b2 = e2
    return a1 * a2, a2 * b1 + b2


def linear_scan(a, b, h0, reverse):
    a_cum, h = lax.associative_scan(_combine, (a, b), reverse=reverse, axis=1)
    if h0 is not None:
        h = h + a_cum * h0[:, None, :]
    return h


def rglru_mixer(x_lat, gate_lat, x_ctx, gate_ctx, conv_w, conv_b, wa, ba, wx, bx, lam, need_ctx):
    xl = centred_dwconv(x_lat, conv_w, conv_b)
    xc = centred_dwconv(x_ctx, conv_w, conv_b)
    h_lat, h_ctx = [], []
    for d, rev in enumerate((False, True)):
        a_c, b_c = rglru_coeffs(xc, wa[d], ba[d], wx[d], bx[d], lam[d])
        hc = linear_scan(a_c, b_c, None, rev)
        h_end = hc[:, 0] if rev else hc[:, -1]
        a_l, b_l = rglru_coeffs(xl, wa[d], ba[d], wx[d], bx[d], lam[d])
        h_lat.append(linear_scan(a_l, b_l, h_end, rev))
        if need_ctx:
            h_ctx.append(hc)
    y_lat = jax.nn.gelu(gate_lat) * (h_lat[0] + h_lat[1]).astype(gate_lat.dtype)
    y_ctx = jax.nn.gelu(gate_ctx) * (h_ctx[0] + h_ctx[1]).astype(gate_ctx.dtype) if need_ctx else None
    return y_lat, y_ctx


def moe_ffn(h, router_w, router_b, w_gu, b_gu, w_down, b_down):
    T, D = h.shape
    logits = (h @ router_w + router_b).astype(jnp.float32)
    top_v, top_e = lax.top_k(logits, TOP_K)
    gates = jax.nn.softmax(top_v, axis=-1)
    n_assign = T * TOP_K
    e_flat = top_e.reshape(-1)
    tok_flat = jnp.arange(n_assign, dtype=jnp.int32) // TOP_K
    order = jnp.argsort(e_flat)
    e_sorted = e_flat[order]
    counts = jnp.zeros((N_EXPERTS,), jnp.int32).at[e_flat].add(1)
    starts = jnp.cumsum(counts) - counts
    padded = (counts + MOE_BLOCK - 1) // MOE_BLOCK * MOE_BLOCK
    pad_end = jnp.cumsum(padded)
    pad_start = pad_end - padded
    dest = pad_start[e_sorted] + jnp.arange(n_assign, dtype=jnp.int32) - starts[e_sorted]
    n_blocks = (n_assign + N_EXPERTS * (MOE_BLOCK - 1) + MOE_BLOCK - 1) // MOE_BLOCK
    n_rows = n_blocks * MOE_BLOCK
    row_tok = jnp.full((n_rows,), T, jnp.int32).at[dest].set(tok_flat[order])
    row_gate = jnp.zeros((n_rows,), jnp.float32).at[dest].set(gates.reshape(-1)[order])
    block_e = jnp.minimum(jnp.searchsorted(pad_end, jnp.arange(n_blocks, dtype=jnp.int32) * MOE_BLOCK, side='right'), N_EXPERTS - 1)
    h_pad = jnp.concatenate([h, jnp.zeros((1, D), h.dtype)], axis=0)
    xs = h_pad[row_tok].reshape(n_blocks, MOE_BLOCK, D)

    def expert_block(args):
        xb, e = args
        gu = xb @ w_gu[e] + b_gu[e]
        g, u = jnp.split(gu, 2, axis=-1)
        g = jnp.minimum(g, SWIGLU_LIMIT)
        u = jnp.clip(u, -SWIGLU_LIMIT, SWIGLU_LIMIT)
        act = (u + 1.0) * (g * jax.nn.sigmoid(SWIGLU_ALPHA * g))
        return act @ w_down[e] + b_down[e]

    ys = lax.map(expert_block, (xs, block_e)).reshape(n_rows, D)
    out = jnp.zeros((T + 1, D), h.dtype).at[row_tok].add(ys * row_gate[:, None].astype(h.dtype))
    return out[:T]


def setup_inputs(seed: int = 0) -> dict:
    key = jax.random.key(seed)
    ks = iter(jax.random.split(key, 48))

    def nrm(shape, scale):
        return jax.random.normal(next(ks), shape, jnp.float32) * scale

    def gain(shape):
        return 1.0 + nrm(shape, 0.02)

    L, D, E, F = DEPTH, D_MODEL, N_EXPERTS, D_EXPERT
    u = jax.random.uniform(next(ks), (L, 2, GROUP_W), jnp.float32, 0.9, 0.999)
    a0 = u ** (1.0 / LRU_C)
    return {
        'x': nrm((BATCH, SEQ, D), 1.0),
        'c': nrm((BATCH, D), 1.0),
        'ctx': nrm((BATCH, CTX_LEN, D), 1.0),
        'c_ctx': nrm((D,), 1.0),
        'w_mod': nrm((L, D, 6 * D), 0.5 * D ** -0.5),
        'b_mod': nrm((L, 6 * D), 0.02),
        'norm_mix': gain((L, D)),
        'norm_ffn': gain((L, D)),
        'w_in': nrm((L, D, IN_COLS), D ** -0.5),
        'w_out': nrm((L, MIX_W, D), MIX_W ** -0.5),
        'pool_w': nrm((L, len(POOL_WINDOWS), POOL_CH, POOL_CH), POOL_CH ** -0.5),
        'pool_scale': gain((L, GROUP_W)),
        'mla_q_a_norm': gain((L, MLA_Q_RANK)),
        'mla_w_uq': nrm((L, MLA_Q_RANK, MLA_HEADS * MLA_QK), MLA_Q_RANK ** -0.5),
        'mla_kv_a_norm': gain((L, MLA_KV_RANK)),
        'mla_w_ukv': nrm((L, MLA_KV_RANK, MLA_HEADS * (MLA_NOPE + MLA_V)), MLA_KV_RANK ** -0.5),
        'mla_q_norm': gain((L, MLA_QK)),
        'mla_k_norm': gain((L, MLA_QK)),
        'sgu_norm_g': gain((L, GROUP_W)),
        'sgu_norm_b': nrm((L, GROUP_W), 0.02),
        'sgu_ws': nrm((L, SGU_HEADS, SGU_CHUNK, SGU_CHUNK), SGU_CHUNK ** -0.5),
        'sgu_b': gain((L, SGU_HEADS, SGU_CHUNK)),
        'lru_conv_w': nrm((L, CONV_W, GROUP_W), CONV_W ** -0.5),
        'lru_conv_b': nrm((L, GROUP_W), 0.02),
        'lru_wa': nrm((L, 2, LRU_HEADS, LRU_HD, LRU_HD), LRU_HD ** -0.5),
        'lru_ba': nrm((L, 2, GROUP_W), 0.02),
        'lru_wx': nrm((L, 2, LRU_HEADS, LRU_HD, LRU_HD), LRU_HD ** -0.5),
        'lru_bx': nrm((L, 2, GROUP_W), 0.02),
        'lru_lambda': jnp.log(a0) - jnp.log1p(-a0),
        'router_w': nrm((L, D, E), D ** -0.5),
        'router_b': nrm((L, E), 0.01),
        'moe_w_gu': nrm((L, E, D, 2 * F), D ** -0.5),
        'moe_b_gu': nrm((L, E, 2 * F), 0.02),
        'moe_w_down': nrm((L, E, F, D), F ** -0.5),
        'moe_b_down': nrm((L, E, D), 0.02),
    }


def reference(x, c, ctx, c_ctx, w_mod, b_mod, norm_mix, norm_ffn, w_in, w_out, pool_w, pool_scale,
              mla_q_a_norm, mla_w_uq, mla_kv_a_norm, mla_w_ukv, mla_q_norm, mla_k_norm,
              sgu_norm_g, sgu_norm_b, sgu_ws, sgu_b,
              lru_conv_w, lru_conv_b, lru_wa, lru_ba, lru_wx, lru_bx, lru_lambda,
              router_w, router_b, moe_w_gu, moe_b_gu, moe_w_down, moe_b_down):
    B, N, D = x.shape
    rows = N // GRID_W
    cos, sin = axial_rope_tables(rows)
    lat, cx = x, ctx
    for l in range(DEPTH):
        need_ctx = l < DEPTH - 1
        mod_lat = jnp.split(jax.nn.silu(c) @ w_mod[l] + b_mod[l], 6, axis=-1)
        mod_ctx = jnp.split(jax.nn.silu(c_ctx) @ w_mod[l] + b_mod[l], 6, axis=-1)
        sh1, sc1, g1, sh2, sc2, g2 = [m[:, None, :] for m in mod_lat]
        csh1, csc1, cg1, csh2, csc2, cg2 = mod_ctx

        h_lat = modulate(rms_norm(lat, norm_mix[l]), sh1, sc1)
        h_ctx = modulate(rms_norm(cx, norm_mix[l]), csh1, csc1)
        p_lat = split_cols(h_lat @ w_in[l])
        p_ctx = split_cols(h_ctx @ w_in[l])

        a_lat = multiscale_pool(p_lat[0], pool_w[l], pool_scale[l])

        k_l, v_l = mla_kv(p_lat[2], p_lat[3], mla_kv_a_norm[l], mla_w_ukv[l], mla_k_norm[l])
        k_c, v_c = mla_kv(p_ctx[2], p_ctx[3], mla_kv_a_norm[l], mla_w_ukv[l], mla_k_norm[l])
        q_l = rope_tail(mla_q(p_lat[1], mla_q_a_norm[l], mla_w_uq[l], mla_q_norm[l]), cos, sin)
        k_l = rope_tail(k_l, cos, sin)
        b_lat = block_attention(q_l, jnp.concatenate([k_c, k_l], axis=1), jnp.concatenate([v_c, v_l], axis=1))

        c_lat = chunk_sgu(p_lat[4], sgu_norm_g[l], sgu_norm_b[l], sgu_ws[l], sgu_b[l])

        d_lat, d_ctx = rglru_mixer(p_lat[5], p_lat[6], p_ctx[5], p_ctx[6], lru_conv_w[l], lru_conv_b[l],
                                   lru_wa[l], lru_ba[l], lru_wx[l], lru_bx[l], lru_lambda[l], need_ctx)

        y_lat = jnp.concatenate([a_lat, b_lat, c_lat, d_lat], axis=-1) @ w_out[l]
        lat = lat + g1 * y_lat
        if need_ctx:
            a_ctx = multiscale_pool(p_ctx[0], pool_w[l], pool_scale[l])
            q_c = mla_q(p_ctx[1], mla_q_a_norm[l], mla_w_uq[l], mla_q_norm[l])
            b_ctx = block_attention(q_c, k_c, v_c)
            c_ctx_out = chunk_sgu(p_ctx[4], sgu_norm_g[l], sgu_norm_b[l], sgu_ws[l], sgu_b[l])
            y_ctx = jnp.concatenate([a_ctx, b_ctx, c_ctx_out, d_ctx], axis=-1) @ w_out[l]
            cx = cx + cg1 * y_ctx

        f_in = modulate(rms_norm(lat, norm_ffn[l]), sh2, sc2).reshape(B * N, D)
        if need_ctx:
            f_in_ctx = modulate(rms_norm(cx, norm_ffn[l]), csh2, csc2).reshape(B * CTX_LEN, D)
            f_in = jnp.concatenate([f_in, f_in_ctx], axis=0)
        f = moe_ffn(f_in, router_w[l], router_b[l], moe_w_gu[l], moe_b_gu[l], moe_w_down[l], moe_b_down[l])
        lat = lat + g2 * f[:B * N].reshape(B, N, D)
        if need_ctx:
            cx = cx + cg2 * f[B * N:].reshape(B, CTX_LEN, D)
    return lat
```

```python
import functools

import jax
import jax.numpy as jnp
from jax import lax
from jax.experimental import pallas as pl
from jax.experimental.pallas import tpu as pltpu

F32 = jnp.float32
BF16 = jnp.bfloat16
HIGHEST = lax.Precision.HIGHEST

EPS = 1e-6
GRID_W = 64
GROUP_W = 256
POOL_WINDOWS = (2, 4, 8, 16)
POOL_CH = GROUP_W // len(POOL_WINDOWS)
MLA_HEADS = 4
MLA_NOPE = 64
MLA_ROPE = 32
MLA_QK = MLA_NOPE + MLA_ROPE
MLA_V = 64
MLA_Q_RANK = 192
MLA_KV_RANK = 128
ROPE_BASE = 10000.0
SGU_HEADS = 4
SGU_HD = GROUP_W // SGU_HEADS
SGU_CHUNK = 128
LRU_HEADS = 4
LRU_HD = GROUP_W // LRU_HEADS
CONV_W = 4
CONV_LEFT = 1
LRU_C = 8.0
N_EXPERTS = 32
TOP_K = 4
SWIGLU_LIMIT = 7.0
SWIGLU_ALPHA = 1.702

LANE = 128
HEAD_PAD = LANE
TILE = 256
HALO = 16
MOE_BM = 256
VMEM_LIMIT = 56 * 1024 * 1024

C_A = 0
C_QA = C_A + GROUP_W
C_KVA = C_QA + 256
C_KR = C_KVA + MLA_KV_RANK
C_CU = C_KR + MLA_HEADS * HEAD_PAD
C_CV = C_CU + GROUP_W
C_DX = C_CV + GROUP_W
C_DG = C_DX + GROUP_W
IN_COLS_P = C_DG + GROUP_W


def _params(sem, vmem=VMEM_LIMIT):
    return pltpu.CompilerParams(dimension_semantics=sem, vmem_limit_bytes=vmem)


def _rms(x, n):
    return x * lax.rsqrt(jnp.sum(x * x, axis=-1, keepdims=True) * (1.0 / n) + EPS)


def _mod_kernel(c_ref, w_ref, b_ref, o_ref):
    cv = c_ref[...]
    s = cv * jax.nn.sigmoid(cv)
    o_ref[0] = jnp.dot(s, w_ref[0], precision=HIGHEST, preferred_element_type=F32) + b_ref[0]


def _mod_call(crows, w_mod, b_mod):
    L, D, M = w_mod.shape
    bn = 1536
    return pl.pallas_call(
        _mod_kernel,
        grid=(L, M // bn),
        in_specs=[
            pl.BlockSpec((8, D), lambda l, j: (0, 0)),
            pl.BlockSpec((1, D, bn), lambda l, j: (l, 0, j)),
            pl.BlockSpec((1, 1, bn), lambda l, j: (l, 0, j)),
        ],
        out_specs=pl.BlockSpec((1, 8, bn), lambda l, j: (l, 0, j)),
        out_shape=jax.ShapeDtypeStruct((L, 8, M), F32),
        compiler_params=_params(("arbitrary", "arbitrary")),
        name="mod",
    )(crows, w_mod, b_mod.reshape(L, 1, M))


def _rope_heads(x, gain, c_tab, s_tab, lane, scale):
    outs = []
    for h in range(MLA_HEADS):
        xh = x[:, h * HEAD_PAD:(h + 1) * HEAD_PAD]
        xn = _rms(xh, MLA_QK) * gain
        swap = jnp.where(lane < MLA_NOPE + MLA_ROPE // 2,
                         pltpu.roll(xn, HEAD_PAD - MLA_ROPE // 2, axis=1),
                         pltpu.roll(xn, MLA_ROPE // 2, axis=1))
        r = xn * c_tab + swap * s_tab
        if scale != 1.0:
            r = r * scale
        outs.append(r.astype(BF16))
    return jnp.concatenate(outs, axis=1)


def _inproj_kernel(x_ref, m_ref, gn_ref, win_ref, gqa_ref, wuq_ref, gkva_ref, wukv_ref,
                   gq_ref, gk_ref, ct_ref, st_ref, sg_ref, sb_ref, ws_ref, sbias_ref,
                   pa_ref, q_ref, k_ref, v_ref, c_ref, xd_ref, gd_ref):
    x = x_ref[...]
    m = m_ref[0]
    h = _rms(x, x.shape[-1]) * gn_ref[...]
    h = h * (1.0 + m[1:2]) + m[0:1]
    p = jnp.dot(h.astype(BF16), win_ref[...], preferred_element_type=F32)

    pa_ref[...] = p[:, C_A:C_A + GROUP_W]
    xd_ref[...] = p[:, C_DX:C_DX + GROUP_W]
    gd_ref[...] = jax.nn.gelu(p[:, C_DG:C_DG + GROUP_W])

    lane = lax.broadcasted_iota(jnp.int32, (x.shape[0], HEAD_PAD), 1)
    c_tab = ct_ref[...]
    s_tab = st_ref[...]
    qa = _rms(p[:, C_QA:C_QA + 256], MLA_Q_RANK) * gqa_ref[...]
    q = jnp.dot(qa.astype(BF16), wuq_ref[...], preferred_element_type=F32)
    q_ref[...] = _rope_heads(q, gq_ref[...], c_tab, s_tab, lane, MLA_QK ** -0.5)

    kva = _rms(p[:, C_KVA:C_KVA + MLA_KV_RANK], MLA_KV_RANK) * gkva_ref[...]
    kv = jnp.dot(kva.astype(BF16), wukv_ref[...], preferred_element_type=F32)
    kpre = kv[:, :MLA_HEADS * HEAD_PAD] + p[:, C_KR:C_KR + MLA_HEADS * HEAD_PAD]
    k_ref[...] = _rope_heads(kpre, gk_ref[...], c_tab, s_tab, lane, 1.0)
    v_ref[...] = kv[:, MLA_HEADS * HEAD_PAD:].astype(BF16)

    z = jax.nn.gelu(p[:, C_CU:C_CU + 2 * GROUP_W])
    u = z[:, :GROUP_W]
    v = z[:, GROUP_W:]
    mu = jnp.mean(v, axis=-1, keepdims=True)
    vc = v - mu
    var = jnp.mean(vc * vc, axis=-1, keepdims=True)
    vn = (vc * lax.rsqrt(var + EPS) * sg_ref[...] + sb_ref[...]).astype(BF16)
    head_of_lane = lax.broadcasted_iota(jnp.int32, (SGU_CHUNK, GROUP_W), 1) // SGU_HD
    zero = jnp.zeros((SGU_CHUNK, GROUP_W), BF16)
    for cch in range(x.shape[0] // SGU_CHUNK):
        rows = slice(cch * SGU_CHUNK, (cch + 1) * SGU_CHUNK)
        vch = vn[rows]
        stacked = jnp.concatenate(
            [jnp.where(head_of_lane == hh, vch, zero) for hh in range(SGU_HEADS)], axis=0)
        mixed = jnp.dot(ws_ref[...], stacked, preferred_element_type=F32) + sbias_ref[...]
        c_ref[rows, :] = (u[rows] * mixed).astype(BF16)


def _inproj_call(xcat, mods, gn, win, gqa, wuq, gkva, wukv, gq, gk, ctab, stab, sg, sb, wscat, sbias,
                 nt_lat):
    T, D = xcat.shape
    nt = T // TILE
    row = lambda i: (i, 0)
    fixed = lambda i: (0, 0)

    def full(a):
        return pl.BlockSpec(a.shape, fixed)

    out_shapes = (
        jax.ShapeDtypeStruct((T, GROUP_W), F32),
        jax.ShapeDtypeStruct((T, MLA_HEADS * HEAD_PAD), BF16),
        jax.ShapeDtypeStruct((T, MLA_HEADS * HEAD_PAD), BF16),
        jax.ShapeDtypeStruct((T, MLA_HEADS * HEAD_PAD), BF16),
        jax.ShapeDtypeStruct((T, GROUP_W), BF16),
        jax.ShapeDtypeStruct((T, GROUP_W), F32),
        jax.ShapeDtypeStruct((T, GROUP_W), F32),
    )
    return pl.pallas_call(
        _inproj_kernel,
        grid=(nt,),
        in_specs=[
            pl.BlockSpec((TILE, D), row),
            pl.BlockSpec((1, 6, D), lambda i: (i // nt_lat, 0, 0)),
            full(gn), full(win), full(gqa), full(wuq), full(gkva), full(wukv), full(gq), full(gk),
            pl.BlockSpec((TILE, HEAD_PAD), row),
            pl.BlockSpec((TILE, HEAD_PAD), row),
            full(sg), full(sb), full(wscat), full(sbias),
        ],
        out_specs=tuple(pl.BlockSpec((TILE, s.shape[1]), row) for s in out_shapes),
        out_shape=out_shapes,
        compiler_params=_params(("parallel",)),
        name="inproj",
    )(xcat, mods, gn, win, gqa, wuq, gkva, wukv, gq, gk, ctab, stab, sg, sb, wscat, sbias)


def _shift_rows(x, d, fill, reverse):
    n = x.shape[0]
    t = lax.broadcasted_iota(jnp.int32, x.shape, 0)
    if reverse:
        return jnp.where(t < n - d, pltpu.roll(x, n - d, axis=0), fill)
    return jnp.where(t >= d, pltpu.roll(x, d, axis=0), fill)


def _tile_scan(a, b, reverse):
    d = 1
    while d < a.shape[0]:
        a_s = _shift_rows(a, d, 1.0, reverse)
        b_s = _shift_rows(b, d, 0.0, reverse)
        b = a * b_s + b
        a = a * a_s
        d *= 2
    return a, b


def _lru_coeffs(xc, proj, sp, d):
    r = jax.nn.sigmoid(proj[:, (2 * d) * GROUP_W:(2 * d + 1) * GROUP_W])
    i = jax.nn.sigmoid(proj[:, (2 * d + 1) * GROUP_W:(2 * d + 2) * GROUP_W])
    log_a = -LRU_C * r * sp[d:d + 1]
    a = jnp.exp(log_a)
    drive = jnp.sqrt(1.0 - jnp.exp(2.0 * log_a)) * (i * xc)
    return a, drive


def _seq_fwd_kernel(nt_lat, n_lat, n_ctx,
                    pa_ref, pa_prev_ref, pa_next_ref, xd_ref, xd_prev_ref, xd_next_ref,
                    wpool_ref, pscale_ref, cw_ref, cb_ref, wlru_ref, blru_ref, lam_ref,
                    a_out_ref, hf_ref, ab_ref, bb_ref,
                    ext_ref, carry_ref):
    j = pl.program_id(0)
    nt = pl.num_programs(0)
    ti = (j + nt_lat) % nt
    is_ctx = ti >= nt_lat
    seq_first = jnp.logical_or(ti == 0, ti == nt_lat)
    seq_last = jnp.logical_or(ti == nt_lat - 1, ti == nt - 1)
    t_loc = (ti - jnp.where(is_ctx, nt_lat, 0)) * TILE
    n_seq = jnp.where(is_ctx, n_ctx, n_lat)

    @pl.when(j == 0)
    def _():
        carry_ref[...] = jnp.zeros_like(carry_ref)

    def load_ext(cur_ref, prev_ref, next_ref):
        ext_ref[0:HALO, :] = jnp.where(seq_first, 0.0, prev_ref[...])
        ext_ref[HALO:HALO + TILE, :] = cur_ref[...]
        ext_ref[HALO + TILE:, :] = jnp.where(seq_last, 0.0, next_ref[...])

    def win(off):
        return ext_ref[pl.ds(HALO + off, TILE), :]

    load_ext(pa_ref, pa_prev_ref, pa_next_ref)
    x = pa_ref[...]
    t = t_loc + lax.broadcasted_iota(jnp.int32, (TILE, GROUP_W), 0)
    lane = lax.broadcasted_iota(jnp.int32, (TILE, GROUP_W), 1)
    mean = jnp.zeros((TILE, GROUP_W), F32)
    for g, w in enumerate(POOL_WINDOWS):
        acc = win(-(w // 2))
        for off in range(-(w // 2) + 1, w // 2):
            acc = acc + win(off)
        cnt = jnp.minimum(t + w // 2, n_seq) - jnp.maximum(t - w // 2, 0)
        mean = jnp.where(lane // POOL_CH == g, acc / cnt.astype(F32), mean)
    diff = (mean - x).astype(BF16)
    pooled = jnp.dot(diff, wpool_ref[...], preferred_element_type=F32) * pscale_ref[...]
    a_out_ref[...] = pooled.astype(BF16)

    load_ext(xd_ref, xd_prev_ref, xd_next_ref)
    xc = jnp.zeros((TILE, GROUP_W), F32) + cb_ref[...]
    for k in range(CONV_W):
        xc = xc + win(k - CONV_LEFT) * cw_ref[k:k + 1, :]

    proj = jnp.dot(xc.astype(BF16), wlru_ref[...], preferred_element_type=F32) + blru_ref[...]
    lam = lam_ref[...]
    sp = jnp.maximum(-lam, 0.0) + jnp.log(1.0 + jnp.exp(-jnp.abs(lam)))
    a_f, b_f = _lru_coeffs(xc, proj, sp, 0)
    a_b, b_b = _lru_coeffs(xc, proj, sp, 1)
    ab_ref[...] = a_b
    bb_ref[...] = b_b

    big_a, big_b = _tile_scan(a_f, b_f, reverse=False)
    hf = big_b + big_a * carry_ref[0:1, :]
    hf_ref[...] = hf
    carry_ref[0:1, :] = hf[TILE - 1:TILE, :]


def _seq_fwd_call(pa, xd, wpool, pscale, cw, cb, wlru, blru, lam, nt_lat, n_lat, n_ctx):
    T = pa.shape[0]
    nt = T // TILE
    hb = TILE // HALO
    n_halo = T // HALO

    def tile_of(j):
        return (j + nt_lat) % nt

    cur = lambda j: (tile_of(j), 0)
    prev = lambda j: (jnp.maximum(tile_of(j) * hb - 1, 0), 0)
    nxt = lambda j: (jnp.minimum((tile_of(j) + 1) * hb, n_halo - 1), 0)
    fixed = lambda j: (0, 0)

    def full(a):
        return pl.BlockSpec(a.shape, fixed)

    tile_spec = pl.BlockSpec((TILE, GROUP_W), cur)
    out_shapes = (
        jax.ShapeDtypeStruct((T, GROUP_W), BF16),
        jax.ShapeDtypeStruct((T, GROUP_W), F32),
        jax.ShapeDtypeStruct((T, GROUP_W), F32),
        jax.ShapeDtypeStruct((T, GROUP_W), F32),
    )
    return pl.pallas_call(
        functools.partial(_seq_fwd_kernel, nt_lat, n_lat, n_ctx),
        grid=(nt,),
        in_specs=[
            tile_spec, pl.BlockSpec((HALO, GROUP_W), prev), pl.BlockSpec((HALO, GROUP_W), nxt),
            tile_spec, pl.BlockSpec((HALO, GROUP_W), prev), pl.BlockSpec((HALO, GROUP_W), nxt),
            full(wpool), full(pscale), full(cw), full(cb), full(wlru), full(blru), full(lam),
        ],
        out_specs=tuple(pl.BlockSpec((TILE, GROUP_W), cur) for _ in out_shapes),
        out_shape=out_shapes,
        scratch_shapes=[pltpu.VMEM((TILE + 2 * HALO, GROUP_W), F32), pltpu.VMEM((8, GROUP_W), F32)],
        compiler_params=_params(("arbitrary",)),
        name="seq_fwd",
    )(pa, pa, pa, xd, xd, xd, wpool, pscale, cw, cb, wlru, blru, lam)


def _seq_bwd_kernel(ab_ref, bb_ref, hf_ref, gd_ref, d_out_ref, carry_ref):
    @pl.when(pl.program_id(0) == 0)
    def _():
        carry_ref[...] = jnp.zeros_like(carry_ref)

    big_a, big_b = _tile_scan(ab_ref[...], bb_ref[...], reverse=True)
    hb = big_b + big_a * carry_ref[0:1, :]
    carry_ref[0:1, :] = hb[0:1, :]
    d_out_ref[...] = (gd_ref[...] * (hf_ref[...] + hb)).astype(BF16)


def _seq_bwd_call(ab, bb, hf, gd):
    T = ab.shape[0]
    nt = T // TILE
    spec = pl.BlockSpec((TILE, GROUP_W), lambda j: (nt - 1 - j, 0))
    return pl.pallas_call(
        _seq_bwd_kernel,
        grid=(nt,),
        in_specs=[spec, spec, spec, spec],
        out_specs=spec,
        out_shape=jax.ShapeDtypeStruct((T, GROUP_W), BF16),
        scratch_shapes=[pltpu.VMEM((8, GROUP_W), F32)],
        compiler_params=_params(("arbitrary",)),
        name="seq_bwd",
    )(ab, bb, hf, gd)


def _attn_kernel(q_ref, k_ref, v_ref, o_ref, m_ref, l_ref, acc_ref):
    kv = pl.program_id(1)

    @pl.when(kv == 0)
    def _():
        m_ref[...] = jnp.full_like(m_ref, -jnp.inf)
        l_ref[...] = jnp.zeros_like(l_ref)
        acc_ref[...] = jnp.zeros_like(acc_ref)

    for h in range(MLA_HEADS):
        cols = slice(h * HEAD_PAD, (h + 1) * HEAD_PAD)
        s = lax.dot_general(q_ref[:, cols], k_ref[:, cols], (((1,), (1,)), ((), ())),
                            preferred_element_type=F32)
        m_prev = m_ref[h]
        m_new = jnp.maximum(m_prev, jnp.max(s, axis=-1, keepdims=True))
        alpha = jnp.exp(m_prev - m_new)
        p = jnp.exp(s - m_new)
        l_ref[h] = alpha * l_ref[h] + jnp.sum(p, axis=-1, keepdims=True)
        acc_ref[h] = alpha * acc_ref[h] + jnp.dot(p.astype(BF16), v_ref[:, cols],
                                                  preferred_element_type=F32)
        m_ref[h] = m_new

    @pl.when(kv == pl.num_programs(1) - 1)
    def _():
        outs = [(acc_ref[h] / l_ref[h])[:, :MLA_V] for h in range(MLA_HEADS)]
        o_ref[...] = jnp.concatenate(outs, axis=1).astype(BF16)


def _attn_call(q, k, v, n_q, tq, q_blk0, n_k, tk, k_blk0):
    width = MLA_HEADS * HEAD_PAD
    return pl.pallas_call(
        _attn_kernel,
        grid=(n_q // tq, n_k // tk),
        in_specs=[
            pl.BlockSpec((tq, width), lambda i, j: (q_blk0 + i, 0)),
            pl.BlockSpec((tk, width), lambda i, j: (k_blk0 + j, 0)),
            pl.BlockSpec((tk, width), lambda i, j: (k_blk0 + j, 0)),
        ],
        out_specs=pl.BlockSpec((tq, MLA_HEADS * MLA_V), lambda i, j: (i, 0)),
        out_shape=jax.ShapeDtypeStruct((n_q, MLA_HEADS * MLA_V), BF16),
        scratch_shapes=[
            pltpu.VMEM((MLA_HEADS, tq, 1), F32),
            pltpu.VMEM((MLA_HEADS, tq, 1), F32),
            pltpu.VMEM((MLA_HEADS, tq, HEAD_PAD), F32),
        ],
        compiler_params=_params(("parallel", "arbitrary")),
        name="attn",
    )(q, k, v)


def _outproj_kernel(a_ref, b_ref, c_ref, d_ref, x_ref, m_ref, wout_ref, gn_ref, rwt_ref, rb_ref, tri_ref,
                    lat_ref, fin_ref, tope_ref, gatet_ref, rank_ref, cnt_ref,
                    carry_ref):
    @pl.when(pl.program_id(0) == 0)
    def _():
        carry_ref[...] = jnp.zeros_like(carry_ref)

    m = m_ref[0]
    y = jnp.dot(jnp.concatenate([a_ref[...], b_ref[...], c_ref[...], d_ref[...]], axis=1),
                wout_ref[...], preferred_element_type=F32)
    lat = x_ref[...] + m[2:3] * y
    lat_ref[...] = lat
    f = _rms(lat, lat.shape[-1]) * gn_ref[...]
    f = f * (1.0 + m[4:5]) + m[3:4]
    fin_ref[...] = f

    logits = lax.dot_general(rwt_ref[...], f, (((1,), (1,)), ((), ())),
                             precision=HIGHEST, preferred_element_type=F32) + rb_ref[...]
    n_e, tm = logits.shape
    e_iota = lax.broadcasted_iota(jnp.int32, (n_e, tm), 0)
    vals, idxs, hots = [], [], []
    for _ in range(TOP_K):
        mx = jnp.max(logits, axis=0, keepdims=True)
        idx = jnp.min(jnp.where(logits == mx, e_iota, n_e), axis=0, keepdims=True)
        hot = e_iota == idx
        vals.append(mx)
        idxs.append(idx)
        hots.append(hot)
        logits = jnp.where(hot, -jnp.inf, logits)
    ex = [jnp.exp(vv - vals[0]) for vv in vals]
    den = ex[0] + ex[1] + ex[2] + ex[3]
    row8 = lax.broadcasted_iota(jnp.int32, (8, tm), 0)
    row128 = lax.broadcasted_iota(jnp.int32, (LANE, tm), 0)
    tope = jnp.zeros((8, tm), jnp.int32)
    gates = jnp.zeros((LANE, tm), F32)
    for k in range(TOP_K):
        tope = jnp.where(row8 == k, idxs[k], tope)
        gates = jnp.where(row128 == k, ex[k] / den, gates)
    tope_ref[...] = tope
    gatet_ref[...] = gates.T

    onehot = jnp.concatenate([jnp.where(hh, 1.0, 0.0) for hh in hots], axis=0)
    prefix = jnp.dot(onehot.astype(BF16), tri_ref[...], preferred_element_type=F32)
    base = carry_ref[:, 0:1]
    ranks = []
    for k in range(TOP_K):
        hot_f = onehot[k * n_e:(k + 1) * n_e]
        ranks.append(jnp.sum(hot_f * (prefix[k * n_e:(k + 1) * n_e] + base), axis=0, keepdims=True))
        base = base + jnp.sum(hot_f, axis=1, keepdims=True)
    rank = jnp.zeros((8, tm), F32)
    for k in range(TOP_K):
        rank = jnp.where(row8 == k, ranks[k], rank)
    rank_ref[...] = rank.astype(jnp.int32)
    carry_ref[...] = jnp.broadcast_to(base, carry_ref.shape)
    cnt_ref[...] = jnp.broadcast_to(base, cnt_ref.shape)


def _outproj_call(a, b, c, d, xcat, mods, wout, gn, rwt, rb, tri, nt_lat):
    T, D = xcat.shape
    nt = T // TILE
    n_e = rwt.shape[0]
    row = lambda i: (i, 0)
    col = lambda i: (0, i)
    fixed = lambda i: (0, 0)

    def full(arr):
        return pl.BlockSpec(arr.shape, fixed)

    g_spec = pl.BlockSpec((TILE, GROUP_W), row)
    out_shapes = (
        jax.ShapeDtypeStruct((T, D), F32),
        jax.ShapeDtypeStruct((T, D), F32),
        jax.ShapeDtypeStruct((8, T), jnp.int32),
        jax.ShapeDtypeStruct((T, LANE), F32),
        jax.ShapeDtypeStruct((8, T), jnp.int32),
        jax.ShapeDtypeStruct((n_e, LANE), F32),
    )
    out_specs = (
        pl.BlockSpec((TILE, D), row), pl.BlockSpec((TILE, D), row),
        pl.BlockSpec((8, TILE), col),
        pl.BlockSpec((TILE, LANE), row), pl.BlockSpec((8, TILE), col),
        pl.BlockSpec((n_e, LANE), fixed),
    )
    return pl.pallas_call(
        _outproj_kernel,
        grid=(nt,),
        in_specs=[
            g_spec, g_spec, g_spec, g_spec,
            pl.BlockSpec((TILE, D), row),
            pl.BlockSpec((1, 6, D), lambda i: (i // nt_lat, 0, 0)),
            full(wout), full(gn), full(rwt), full(rb), full(tri),
        ],
        out_specs=out_specs,
        out_shape=out_shapes,
        scratch_shapes=[pltpu.VMEM((n_e, LANE), F32)],
        compiler_params=_params(("arbitrary",)),
        name="outproj",
    )(a, b, c, d, xcat, mods, wout, gn, rwt, rb, tri)


def _row_copy(src, s, dst, d, sem):
    return pltpu.make_async_copy(src.at[pl.ds(s, 1)], dst.at[pl.ds(d, 1)], sem)


def _dispatch_kernel(dest_ref, fin_ref, xs_in_ref, xs_ref, sem):
    del xs_in_ref
    base = pl.program_id(0) * TILE

    def issue(r, carry):
        for k in range(TOP_K):
            _row_copy(fin_ref, base + r, xs_ref, dest_ref[0, 0, k * TILE + r], sem).start()
        return carry

    lax.fori_loop(0, TILE, issue, 0)

    def drain(r, carry):
        for k in range(TOP_K):
            _row_copy(fin_ref, 0, xs_ref, 0, sem).wait()
        return carry

    lax.fori_loop(0, TILE, drain, 0)


def _dispatch_call(dest3, fin, xs_zero):
    nt = dest3.shape[0]
    return pl.pallas_call(
        _dispatch_kernel,
        grid=(nt,),
        in_specs=[
            pl.BlockSpec((1, 1, TOP_K * TILE), lambda i: (i, 0, 0), memory_space=pltpu.SMEM),
            pl.BlockSpec(memory_space=pl.ANY),
            pl.BlockSpec(memory_space=pl.ANY),
        ],
        out_specs=pl.BlockSpec(memory_space=pl.ANY),
        out_shape=jax.ShapeDtypeStruct(xs_zero.shape, xs_zero.dtype),
        scratch_shapes=[pltpu.SemaphoreType.DMA(())],
        input_output_aliases={2: 0},
        compiler_params=_params(("arbitrary",)),
        name="dispatch",
    )(dest3, fin, xs_zero)


def _expert_kernel(be_ref, nu_ref, xs_ref, wgu_ref, bgu_ref, wdn_ref, bdn_ref, ys_ref, wgu_bf, wdn_bf):
    i = pl.program_id(0)
    changed = jnp.logical_or(i == 0, be_ref[i] != be_ref[jnp.maximum(i - 1, 0)])

    @pl.when(changed)
    def _():
        wgu_bf[...] = wgu_ref[...].astype(BF16)
        wdn_bf[...] = wdn_ref[...].astype(BF16)

    @pl.when(i < nu_ref[0])
    def _():
        f = wdn_bf.shape[0]
        gu = jnp.dot(xs_ref[...].astype(BF16), wgu_bf[...], preferred_element_type=F32) + bgu_ref[...]
        g = jnp.minimum(gu[:, :f], SWIGLU_LIMIT)
        u = jnp.clip(gu[:, f:], -SWIGLU_LIMIT, SWIGLU_LIMIT)
        act = (u + 1.0) * (g * jax.nn.sigmoid(SWIGLU_ALPHA * g))
        ys_ref[...] = jnp.dot(act.astype(BF16), wdn_bf[...], preferred_element_type=F32) + bdn_ref[...]

    @pl.when(i >= nu_ref[0])
    def _():
        ys_ref[...] = jnp.zeros_like(ys_ref)


def _expert_call(block_e, n_used, xs, w_gu, b_gu, w_down, b_down, layer):
    n_rows, D = xs.shape
    _, E, _, F2 = w_gu.shape
    F = F2 // 2
    grid_spec = pltpu.PrefetchScalarGridSpec(
        num_scalar_prefetch=2,
        grid=(n_rows // MOE_BM,),
        in_specs=[
            pl.BlockSpec((MOE_BM, D), lambda i, be, nu: (i, 0)),
            pl.BlockSpec((None, None, D, F2), lambda i, be, nu: (layer, be[i], 0, 0)),
            pl.BlockSpec((None, None, 1, F2), lambda i, be, nu: (layer, be[i], 0, 0)),
            pl.BlockSpec((None, None, F, D), lambda i, be, nu: (layer, be[i], 0, 0)),
            pl.BlockSpec((None, None, 1, D), lambda i, be, nu: (layer, be[i], 0, 0)),
        ],
        out_specs=pl.BlockSpec((MOE_BM, D), lambda i, be, nu: (i, 0)),
        scratch_shapes=[pltpu.VMEM((D, F2), BF16), pltpu.VMEM((F, D), BF16)],
    )
    L = w_gu.shape[0]
    return pl.pallas_call(
        _expert_kernel,
        grid_spec=grid_spec,
        out_shape=jax.ShapeDtypeStruct((n_rows, D), F32),
        compiler_params=_params(("arbitrary",)),
        name="expert",
    )(block_e, n_used, xs, w_gu, b_gu.reshape(L, E, 1, F2), w_down, b_down.reshape(L, E, 1, D))


def _combine_kernel(dest_ref, ys_ref, gt_ref, lat_ref, m_ref, out_ref, buf_ref, sem):
    def issue(r, carry):
        for k in range(TOP_K):
            _row_copy(ys_ref, dest_ref[0, 0, k * TILE + r], buf_ref.at[k], r, sem).start()
        return carry

    lax.fori_loop(0, TILE, issue, 0)

    def drain(r, carry):
        for k in range(TOP_K):
            _row_copy(ys_ref, 0, buf_ref.at[k], 0, sem).wait()
        return carry

    lax.fori_loop(0, TILE, drain, 0)

    gt = gt_ref[...]
    f = gt[:, 0:1] * buf_ref[0]
    for k in range(1, TOP_K):
        f = f + gt[:, k:k + 1] * buf_ref[k]
    out_ref[...] = lat_ref[...] + m_ref[0][5:6] * f


def _combine_call(dest3, ys, gates_t, lat, mods, nt_lat):
    T, D = lat.shape
    nt = T // TILE
    row = lambda i: (i, 0)
    return pl.pallas_call(
        _combine_kernel,
        grid=(nt,),
        in_specs=[
            pl.BlockSpec((1, 1, TOP_K * TILE), lambda i: (i, 0, 0), memory_space=pltpu.SMEM),
            pl.BlockSpec(memory_space=pl.ANY),
            pl.BlockSpec((TILE, LANE), row),
            pl.BlockSpec((TILE, D), row),
            pl.BlockSpec((1, 6, D), lambda i: (i // nt_lat, 0, 0)),
        ],
        out_specs=pl.BlockSpec((TILE, D), row),
        out_shape=jax.ShapeDtypeStruct((T, D), F32),
        scratch_shapes=[pltpu.VMEM((TOP_K, TILE, D), F32), pltpu.SemaphoreType.DMA(())],
        compiler_params=_params(("arbitrary",)),
        name="combine",
    )(dest3, ys, gates_t, lat, mods)


def _block_diag(blocks):
    H, a, b = blocks.shape
    eye = jnp.eye(H, dtype=blocks.dtype)
    return (eye[:, None, :, None] * blocks[:, :, None, :]).reshape(H * a, H * b)


def _head_slabs(w, width):
    lead = w.shape[:-1]
    w = w.reshape(lead + (MLA_HEADS, width))
    w = jnp.pad(w, [(0, 0)] * len(lead) + [(0, 0), (0, HEAD_PAD - width)])
    return w.reshape(lead + (MLA_HEADS * HEAD_PAD,))


def _rope_tables(rows, n_ctx):
    n_freq = MLA_ROPE // 4
    inv = ROPE_BASE ** (-jnp.arange(n_freq, dtype=F32) / n_freq)
    r = jnp.repeat(jnp.arange(rows, dtype=F32), GRID_W)
    col = jnp.tile(jnp.arange(GRID_W, dtype=F32), rows)
    ang = jnp.concatenate([r[:, None] * inv, col[:, None] * inv], axis=-1)
    cos, sin = jnp.cos(ang), jnp.sin(ang)
    n = cos.shape[0]
    half = MLA_ROPE // 2
    pad = HEAD_PAD - MLA_QK
    c_tab = jnp.concatenate([jnp.ones((n, MLA_NOPE), F32), cos, cos, jnp.ones((n, pad), F32)], axis=1)
    s_tab = jnp.concatenate([jnp.zeros((n, MLA_NOPE), F32), -sin, sin, jnp.zeros((n, pad), F32)], axis=1)
    c_tab = jnp.concatenate([c_tab, jnp.ones((n_ctx, HEAD_PAD), F32)], axis=0)
    s_tab = jnp.concatenate([s_tab, jnp.zeros((n_ctx, HEAD_PAD), F32)], axis=0)
    del half
    return c_tab, s_tab


def kernel(x, c, ctx, c_ctx, w_mod, b_mod, norm_mix, norm_ffn, w_in, w_out, pool_w, pool_scale, mla_q_a_norm, mla_w_uq, mla_kv_a_norm, mla_w_ukv, mla_q_norm, mla_k_norm, sgu_norm_g, sgu_norm_b, sgu_ws, sgu_b, lru_conv_w, lru_conv_b, lru_wa, lru_ba, lru_wx, lru_bx, lru_lambda, router_w, router_b, moe_w_gu, moe_b_gu, moe_w_down, moe_b_down):
    B, N, D = x.shape
    n_ctx = ctx.shape[1]
    assert B == 1 and N % TILE == 0 and n_ctx == TILE
    L = w_mod.shape[0]
    T = N + n_ctx
    nt = T // TILE
    nt_lat = N // TILE

    lat = jnp.concatenate([x[0], ctx[0]], axis=0)
    crows = jnp.zeros((8, D), F32).at[0].set(c[0]).at[1].set(c_ctx)
    mods_all = _mod_call(crows, w_mod, b_mod)
    c_tab, s_tab = _rope_tables(N // GRID_W, n_ctx)
    tri = jnp.triu(jnp.ones((TILE, TILE), F32), k=1).astype(BF16)

    n_assign = T * TOP_K
    n_blocks = -(-(n_assign + N_EXPERTS * (MOE_BM - 1)) // MOE_BM)
    n_rows = n_blocks * MOE_BM

    for l in range(L):
        mods = mods_all[l, :2].reshape(2, 6, D)

        wa_, wqa, wkva, wkr, wc_, wdx, wdg = jnp.split(
            w_in[l], [256, 448, 576, 608, 1120, 1376], axis=1)
        wkr_placed = jnp.pad(wkr.reshape(D, 1, MLA_ROPE),
                             ((0, 0), (0, 0), (MLA_NOPE, HEAD_PAD - MLA_QK)))
        wkr_placed = jnp.tile(wkr_placed, (1, MLA_HEADS, 1)).reshape(D, MLA_HEADS * HEAD_PAD)
        win = jnp.concatenate(
            [wa_, jnp.pad(wqa, ((0, 0), (0, 256 - MLA_Q_RANK))), wkva, wkr_placed, wc_, wdx, wdg],
            axis=1).astype(BF16)
        assert win.shape[1] == IN_COLS_P
        gqa = jnp.pad(mla_q_a_norm[l], (0, 256 - MLA_Q_RANK)).reshape(1, 256)
        wuq = jnp.pad(_head_slabs(mla_w_uq[l], MLA_QK), ((0, 256 - MLA_Q_RANK), (0, 0))).astype(BF16)
        gkva = mla_kv_a_norm[l].reshape(1, MLA_KV_RANK)
        wukv3 = mla_w_ukv[l].reshape(MLA_KV_RANK, MLA_HEADS, MLA_NOPE + MLA_V)
        wuk = _head_slabs(wukv3[:, :, :MLA_NOPE].reshape(MLA_KV_RANK, -1), MLA_NOPE)
        wuv = _head_slabs(wukv3[:, :, MLA_NOPE:].reshape(MLA_KV_RANK, -1), MLA_V)
        wukv = jnp.concatenate([wuk, wuv], axis=1).astype(BF16)
        gq = jnp.pad(mla_q_norm[l], (0, HEAD_PAD - MLA_QK)).reshape(1, HEAD_PAD)
        gk = jnp.pad(mla_k_norm[l], (0, HEAD_PAD - MLA_QK)).reshape(1, HEAD_PAD)
        wscat = sgu_ws[l].transpose(1, 0, 2).reshape(SGU_CHUNK, SGU_HEADS * SGU_CHUNK).astype(BF16)
        sbias = jnp.repeat(sgu_b[l].T, SGU_HD, axis=1)

        pa, q, k, v, c_s, xd, gd = _inproj_call(
            lat, mods, norm_mix[l].reshape(1, D), win, gqa, wuq, gkva, wukv, gq, gk, c_tab, s_tab,
            sgu_norm_g[l].reshape(1, GROUP_W), sgu_norm_b[l].reshape(1, GROUP_W), wscat, sbias, nt_lat)

        wpool = _block_diag(pool_w[l]).astype(BF16)
        wlru = jnp.concatenate(
            [_block_diag(lru_wa[l, 0]), _block_diag(lru_wx[l, 0]),
             _block_diag(lru_wa[l, 1]), _block_diag(lru_wx[l, 1])], axis=1).astype(BF16)
        blru = jnp.concatenate([lru_ba[l, 0], lru_bx[l, 0], lru_ba[l, 1], lru_bx[l, 1]]).reshape(1, -1)
        a_s, hf, ab, bb = _seq_fwd_call(
            pa, xd, wpool, pool_scale[l].reshape(1, GROUP_W), lru_conv_w[l],
            lru_conv_b[l].reshape(1, GROUP_W), wlru, blru, lru_lambda[l], nt_lat, N, n_ctx)
        d_s = _seq_bwd_call(ab, bb, hf, gd)

        b_lat = _attn_call(q, k, v, N, 512, 0, T, 1280, 0)
        if l < L - 1:
            b_ctx = _attn_call(q, k, v, n_ctx, n_ctx, N // n_ctx, n_ctx, n_ctx, N // n_ctx)
        else:
            b_ctx = jnp.zeros((n_ctx, MLA_HEADS * MLA_V), BF16)
        b_s = jnp.concatenate([b_lat, b_ctx], axis=0)

        lat2, fin, top_e, gates_t, rank, cnt = _outproj_call(
            a_s, b_s, c_s, d_s, lat, mods, w_out[l].astype(BF16), norm_ffn[l].reshape(1, D),
            router_w[l].T, router_b[l].reshape(N_EXPERTS, 1), tri, nt_lat)

        counts = cnt[:, 0].astype(jnp.int32)
        padded = (counts + MOE_BM - 1) // MOE_BM * MOE_BM
        pad_end = jnp.cumsum(padded)
        pad_start = pad_end - padded
        dest = pad_start[top_e[:TOP_K]] + rank[:TOP_K]
        dest3 = dest.reshape(TOP_K, nt, TILE).transpose(1, 0, 2).reshape(nt, 1, TOP_K * TILE)
        block_e = jnp.minimum(
            jnp.searchsorted(pad_end, jnp.arange(n_blocks, dtype=jnp.int32) * MOE_BM, side='right'),
            N_EXPERTS - 1).astype(jnp.int32)
        n_used = (pad_end[-1:] // MOE_BM).astype(jnp.int32)

        xs = _dispatch_call(dest3, fin, jnp.zeros((n_rows, D), F32))
        ys = _expert_call(block_e, n_used, xs, moe_w_gu, moe_b_gu, moe_w_down, moe_b_down, l)
        lat = _combine_call(dest3, ys, gates_t, lat2, mods, nt_lat)

    return lat[:N].reshape(B, N, D)
```

```python
import functools

import jax
import jax.numpy as jnp
from jax import lax
from jax.experimental import pallas as pl
from jax.experimental.pallas import tpu as pltpu

F32 = jnp.float32
BF16 = jnp.bfloat16
HIGHEST = lax.Precision.HIGHEST

EPS = 1e-6
LOG2_E = 1.4426950408889634
GRID_W = 64
GROUP_W = 256
POOL_WINDOWS = (2, 4, 8, 16)
POOL_CH = GROUP_W // len(POOL_WINDOWS)
MLA_HEADS = 4
MLA_NOPE = 64
MLA_ROPE = 32
MLA_QK = MLA_NOPE + MLA_ROPE
MLA_V = 64
MLA_Q_RANK = 192
MLA_KV_RANK = 128
ROPE_BASE = 10000.0
SGU_HEADS = 4
SGU_HD = GROUP_W // SGU_HEADS
SGU_CHUNK = 128
LRU_HEADS = 4
LRU_HD = GROUP_W // LRU_HEADS
CONV_W = 4
CONV_LEFT = 1
LRU_C = 8.0
N_EXPERTS = 32
TOP_K = 4
SWIGLU_LIMIT = 7.0
SWIGLU_ALPHA = 1.702

LANE = 128
HEAD_PAD = LANE
TILE = 256
HALO = 16
MOE_BM = 256
VMEM_LIMIT = 56 * 1024 * 1024

C_A = 0
C_QA = C_A + GROUP_W
C_KVA = C_QA + 256
C_KR = C_KVA + MLA_KV_RANK
C_CU = C_KR + MLA_HEADS * HEAD_PAD
C_CV = C_CU + GROUP_W
C_DX = C_CV + GROUP_W
C_DG = C_DX + GROUP_W
IN_COLS_P = C_DG + GROUP_W


def _params(sem, vmem=VMEM_LIMIT):
    return pltpu.CompilerParams(dimension_semantics=sem, vmem_limit_bytes=vmem)


def _rms(x, n):
    return x * lax.rsqrt(jnp.sum(x * x, axis=-1, keepdims=True) * (1.0 / n) + EPS)


def _mod_kernel(c_ref, w_ref, b_ref, o_ref):
    cv = c_ref[...]
    s = cv * jax.nn.sigmoid(cv)
    o_ref[0] = jnp.dot(s, w_ref[0], precision=HIGHEST, preferred_element_type=F32) + b_ref[0]


def _mod_call(crows, w_mod, b_mod):
    L, D, M = w_mod.shape
    bn = 1536
    return pl.pallas_call(
        _mod_kernel,
        grid=(L, M // bn),
        in_specs=[
            pl.BlockSpec((8, D), lambda l, j: (0, 0)),
            pl.BlockSpec((1, D, bn), lambda l, j: (l, 0, j)),
            pl.BlockSpec((1, 1, bn), lambda l, j: (l, 0, j)),
        ],
        out_specs=pl.BlockSpec((1, 8, bn), lambda l, j: (l, 0, j)),
        out_shape=jax.ShapeDtypeStruct((L, 8, M), F32),
        compiler_params=_params(("arbitrary", "arbitrary")),
        name="mod",
    )(crows, w_mod, b_mod.reshape(L, 1, M))


def _rope_heads(x, gain, c_tab, s_tab, lane, scale):
    outs = []
    for h in range(MLA_HEADS):
        xh = x[:, h * HEAD_PAD:(h + 1) * HEAD_PAD]
        xn = _rms(xh, MLA_QK) * gain
        swap = jnp.where(lane < MLA_NOPE + MLA_ROPE // 2,
                         pltpu.roll(xn, HEAD_PAD - MLA_ROPE // 2, axis=1),
                         pltpu.roll(xn, MLA_ROPE // 2, axis=1))
        r = xn * c_tab + swap * s_tab
        if scale != 1.0:
            r = r * scale
        outs.append(r.astype(BF16))
    return jnp.concatenate(outs, axis=1)


def _inproj_kernel(x_ref, m_ref, gn_ref, win_ref, gqa_ref, wuq_ref, gkva_ref, wukv_ref,
                   gq_ref, gk_ref, ct_ref, st_ref, sg_ref, sb_ref, ws_ref, sbias_ref,
                   pa_ref, q_ref, k_ref, vt_ref, c_ref, xd_ref, gd_ref):
    x = x_ref[...]
    m = m_ref[0]
    h = _rms(x, x.shape[-1]) * gn_ref[...]
    h = h * (1.0 + m[1:2]) + m[0:1]
    p = jnp.dot(h.astype(BF16), win_ref[...], preferred_element_type=F32)

    pa_ref[...] = p[:, C_A:C_A + GROUP_W]
    xd_ref[...] = p[:, C_DX:C_DX + GROUP_W]
    gd_ref[...] = jax.nn.gelu(p[:, C_DG:C_DG + GROUP_W])

    lane = lax.broadcasted_iota(jnp.int32, (x.shape[0], HEAD_PAD), 1)
    c_tab = ct_ref[...]
    s_tab = st_ref[...]
    qa = _rms(p[:, C_QA:C_QA + 256], MLA_Q_RANK) * gqa_ref[...]
    q = jnp.dot(qa.astype(BF16), wuq_ref[...], preferred_element_type=F32)
    q_ref[...] = _rope_heads(q, gq_ref[...], c_tab, s_tab, lane, MLA_QK ** -0.5 * LOG2_E)

    kva = _rms(p[:, C_KVA:C_KVA + MLA_KV_RANK], MLA_KV_RANK) * gkva_ref[...]
    kv = jnp.dot(kva.astype(BF16), wukv_ref[...], preferred_element_type=F32)
    kpre = kv[:, :MLA_HEADS * HEAD_PAD] + p[:, C_KR:C_KR + MLA_HEADS * HEAD_PAD]
    k_ref[...] = _rope_heads(kpre, gk_ref[...], c_tab, s_tab, lane, 1.0)
    v = kv[:, MLA_HEADS * HEAD_PAD:]
    slab_lane = lax.broadcasted_iota(jnp.int32, v.shape, 1) % HEAD_PAD
    vt_ref[...] = jnp.where(slab_lane == MLA_V, 1.0, v).T.astype(BF16)

    z = jax.nn.gelu(p[:, C_CU:C_CU + 2 * GROUP_W])
    u = z[:, :GROUP_W]
    v = z[:, GROUP_W:]
    mu = jnp.mean(v, axis=-1, keepdims=True)
    vc = v - mu
    var = jnp.mean(vc * vc, axis=-1, keepdims=True)
    vn = (vc * lax.rsqrt(var + EPS) * sg_ref[...] + sb_ref[...]).astype(BF16)
    head_of_lane = lax.broadcasted_iota(jnp.int32, (SGU_CHUNK, GROUP_W), 1) // SGU_HD
    zero = jnp.zeros((SGU_CHUNK, GROUP_W), BF16)
    for cch in range(x.shape[0] // SGU_CHUNK):
        rows = slice(cch * SGU_CHUNK, (cch + 1) * SGU_CHUNK)
        vch = vn[rows]
        stacked = jnp.concatenate(
            [jnp.where(head_of_lane == hh, vch, zero) for hh in range(SGU_HEADS)], axis=0)
        mixed = jnp.dot(ws_ref[...], stacked, preferred_element_type=F32) + sbias_ref[...]
        c_ref[rows, :] = (u[rows] * mixed).astype(BF16)


def _inproj_call(xcat, mods, gn, win, gqa, wuq, gkva, wukv, gq, gk, ctab, stab, sg, sb, wscat, sbias,
                 nt_lat):
    T, D = xcat.shape
    nt = T // TILE
    row = lambda i: (i, 0)
    fixed = lambda i: (0, 0)

    def full(a):
        return pl.BlockSpec(a.shape, fixed)

    out_shapes = (
        jax.ShapeDtypeStruct((T, GROUP_W), F32),
        jax.ShapeDtypeStruct((T, MLA_HEADS * HEAD_PAD), BF16),
        jax.ShapeDtypeStruct((T, MLA_HEADS * HEAD_PAD), BF16),
        jax.ShapeDtypeStruct((MLA_HEADS * HEAD_PAD, T), BF16),
        jax.ShapeDtypeStruct((T, GROUP_W), BF16),
        jax.ShapeDtypeStruct((T, GROUP_W), F32),
        jax.ShapeDtypeStruct((T, GROUP_W), F32),
    )
    out_specs = [pl.BlockSpec((TILE, s.shape[1]), row) for s in out_shapes]
    out_specs[3] = pl.BlockSpec((MLA_HEADS * HEAD_PAD, TILE), lambda i: (0, i))
    return pl.pallas_call(
        _inproj_kernel,
        grid=(nt,),
        in_specs=[
            pl.BlockSpec((TILE, D), row),
            pl.BlockSpec((1, 6, D), lambda i: (i // nt_lat, 0, 0)),
            full(gn), full(win), full(gqa), full(wuq), full(gkva), full(wukv), full(gq), full(gk),
            pl.BlockSpec((TILE, HEAD_PAD), row),
            pl.BlockSpec((TILE, HEAD_PAD), row),
            full(sg), full(sb), full(wscat), full(sbias),
        ],
        out_specs=tuple(out_specs),
        out_shape=out_shapes,
        compiler_params=_params(("parallel",)),
        name="inproj",
    )(xcat, mods, gn, win, gqa, wuq, gkva, wukv, gq, gk, ctab, stab, sg, sb, wscat, sbias)


def _shift_rows(x, d, fill, reverse):
    n = x.shape[0]
    t = lax.broadcasted_iota(jnp.int32, x.shape, 0)
    if reverse:
        return jnp.where(t < n - d, pltpu.roll(x, n - d, axis=0), fill)
    return jnp.where(t >= d, pltpu.roll(x, d, axis=0), fill)


def _tile_scan(a, b, reverse):
    d = 1
    while d < a.shape[0]:
        a_s = _shift_rows(a, d, 1.0, reverse)
        b_s = _shift_rows(b, d, 0.0, reverse)
        b = a * b_s + b
        a = a * a_s
        d *= 2
    return a, b


def _lru_coeffs(xc, proj, sp, d):
    r = jax.nn.sigmoid(proj[:, (2 * d) * GROUP_W:(2 * d + 1) * GROUP_W])
    i = jax.nn.sigmoid(proj[:, (2 * d + 1) * GROUP_W:(2 * d + 2) * GROUP_W])
    log_a = -LRU_C * r * sp[d:d + 1]
    a = jnp.exp(log_a)
    drive = jnp.sqrt(1.0 - jnp.exp(2.0 * log_a)) * (i * xc)
    return a, drive


def _seq_fwd_kernel(nt_lat, n_lat, n_ctx,
                    pa_ref, pa_prev_ref, pa_next_ref, xd_ref, xd_prev_ref, xd_next_ref,
                    wpool_ref, pscale_ref, cw_ref, cb_ref, wlru_ref, blru_ref, lam_ref,
                    a_out_ref, hf_ref, ab_ref, bb_ref,
                    ext_ref, carry_ref):
    j = pl.program_id(0)
    nt = pl.num_programs(0)
    ti = (j + nt_lat) % nt
    is_ctx = ti >= nt_lat
    seq_first = jnp.logical_or(ti == 0, ti == nt_lat)
    seq_last = jnp.logical_or(ti == nt_lat - 1, ti == nt - 1)
    t_loc = (ti - jnp.where(is_ctx, nt_lat, 0)) * TILE
    n_seq = jnp.where(is_ctx, n_ctx, n_lat)

    @pl.when(j == 0)
    def _():
        carry_ref[...] = jnp.zeros_like(carry_ref)

    def load_ext(cur_ref, prev_ref, next_ref):
        ext_ref[0:HALO, :] = jnp.where(seq_first, 0.0, prev_ref[...])
        ext_ref[HALO:HALO + TILE, :] = cur_ref[...]
        ext_ref[HALO + TILE:, :] = jnp.where(seq_last, 0.0, next_ref[...])

    def win(off):
        return ext_ref[pl.ds(HALO + off, TILE), :]

    load_ext(pa_ref, pa_prev_ref, pa_next_ref)
    x = pa_ref[...]
    t = t_loc + lax.broadcasted_iota(jnp.int32, (TILE, GROUP_W), 0)
    lane = lax.broadcasted_iota(jnp.int32, (TILE, GROUP_W), 1)
    mean = jnp.zeros((TILE, GROUP_W), F32)
    for g, w in enumerate(POOL_WINDOWS):
        acc = win(-(w // 2))
        for off in range(-(w // 2) + 1, w // 2):
            acc = acc + win(off)
        cnt = jnp.minimum(t + w // 2, n_seq) - jnp.maximum(t - w // 2, 0)
        mean = jnp.where(lane // POOL_CH == g, acc / cnt.astype(F32), mean)
    diff = (mean - x).astype(BF16)
    pooled = jnp.dot(diff, wpool_ref[...], preferred_element_type=F32) * pscale_ref[...]
    a_out_ref[...] = pooled.astype(BF16)

    load_ext(xd_ref, xd_prev_ref, xd_next_ref)
    xc = jnp.zeros((TILE, GROUP_W), F32) + cb_ref[...]
    for k in range(CONV_W):
        xc = xc + win(k - CONV_LEFT) * cw_ref[k:k + 1, :]

    proj = jnp.dot(xc.astype(BF16), wlru_ref[...], preferred_element_type=F32) + blru_ref[...]
    lam = lam_ref[...]
    sp = jnp.maximum(-lam, 0.0) + jnp.log(1.0 + jnp.exp(-jnp.abs(lam)))
    a_f, b_f = _lru_coeffs(xc, proj, sp, 0)
    a_b, b_b = _lru_coeffs(xc, proj, sp, 1)
    ab_ref[...] = a_b
    bb_ref[...] = b_b

    big_a, big_b = _tile_scan(a_f, b_f, reverse=False)
    hf = big_b + big_a * carry_ref[0:1, :]
    hf_ref[...] = hf
    carry_ref[0:1, :] = hf[TILE - 1:TILE, :]


def _seq_fwd_call(pa, xd, wpool, pscale, cw, cb, wlru, blru, lam, nt_lat, n_lat, n_ctx):
    T = pa.shape[0]
    nt = T // TILE
    hb = TILE // HALO
    n_halo = T // HALO

    def tile_of(j):
        return (j + nt_lat) % nt

    cur = lambda j: (tile_of(j), 0)
    prev = lambda j: (jnp.maximum(tile_of(j) * hb - 1, 0), 0)
    nxt = lambda j: (jnp.minimum((tile_of(j) + 1) * hb, n_halo - 1), 0)
    fixed = lambda j: (0, 0)

    def full(a):
        return pl.BlockSpec(a.shape, fixed)

    tile_spec = pl.BlockSpec((TILE, GROUP_W), cur)
    out_shapes = (
        jax.ShapeDtypeStruct((T, GROUP_W), BF16),
        jax.ShapeDtypeStruct((T, GROUP_W), F32),
        jax.ShapeDtypeStruct((T, GROUP_W), F32),
        jax.ShapeDtypeStruct((T, GROUP_W), F32),
    )
    return pl.pallas_call(
        functools.partial(_seq_fwd_kernel, nt_lat, n_lat, n_ctx),
        grid=(nt,),
        in_specs=[
            tile_spec, pl.BlockSpec((HALO, GROUP_W), prev), pl.BlockSpec((HALO, GROUP_W), nxt),
            tile_spec, pl.BlockSpec((HALO, GROUP_W), prev), pl.BlockSpec((HALO, GROUP_W), nxt),
            full(wpool), full(pscale), full(cw), full(cb), full(wlru), full(blru), full(lam),
        ],
        out_specs=tuple(pl.BlockSpec((TILE, GROUP_W), cur) for _ in out_shapes),
        out_shape=out_shapes,
        scratch_shapes=[pltpu.VMEM((TILE + 2 * HALO, GROUP_W), F32), pltpu.VMEM((8, GROUP_W), F32)],
        compiler_params=_params(("arbitrary",)),
        name="seq_fwd",
    )(pa, pa, pa, xd, xd, xd, wpool, pscale, cw, cb, wlru, blru, lam)


def _seq_bwd_kernel(ab_ref, bb_ref, hf_ref, gd_ref, d_out_ref, carry_ref):
    @pl.when(pl.program_id(0) == 0)
    def _():
        carry_ref[...] = jnp.zeros_like(carry_ref)

    big_a, big_b = _tile_scan(ab_ref[...], bb_ref[...], reverse=True)
    hb = big_b + big_a * carry_ref[0:1, :]
    carry_ref[0:1, :] = hb[0:1, :]
    d_out_ref[...] = (gd_ref[...] * (hf_ref[...] + hb)).astype(BF16)


def _seq_bwd_call(ab, bb, hf, gd):
    T = ab.shape[0]
    nt = T // TILE
    spec = pl.BlockSpec((TILE, GROUP_W), lambda j: (nt - 1 - j, 0))
    return pl.pallas_call(
        _seq_bwd_kernel,
        grid=(nt,),
        in_specs=[spec, spec, spec, spec],
        out_specs=spec,
        out_shape=jax.ShapeDtypeStruct((T, GROUP_W), BF16),
        scratch_shapes=[pltpu.VMEM((8, GROUP_W), F32)],
        compiler_params=_params(("arbitrary",)),
        name="seq_bwd",
    )(ab, bb, hf, gd)


def _attn_kernel(q_ref, k_ref, vt_ref, o_ref, m_ref, acc_ref):
    kv = pl.program_id(1)

    @pl.when(kv == 0)
    def _():
        m_ref[...] = jnp.full_like(m_ref, -jnp.inf)
        acc_ref[...] = jnp.zeros_like(acc_ref)

    def head_cols(h):
        return slice(h * HEAD_PAD, (h + 1) * HEAD_PAD)

    def scores(h):
        cols = head_cols(h)
        return lax.dot_general(k_ref[:, cols], q_ref[:, cols], (((1,), (1,)), ((), ())),
                               preferred_element_type=F32)

    def probs(h, s):
        m_prev = m_ref[h]
        m_new = jnp.maximum(m_prev, jnp.max(s, axis=0, keepdims=True))
        m_ref[h] = m_new
        return jnp.exp2(m_prev - m_new), jnp.exp2(s - m_new).astype(BF16)

    def accumulate(h, alpha, p):
        acc_ref[h] = alpha * acc_ref[h] + jnp.dot(vt_ref[head_cols(h), :], p, preferred_element_type=F32)

    s_next = scores(0)
    pending = None
    for h in range(MLA_HEADS):
        s_cur = s_next
        if h + 1 < MLA_HEADS:
            s_next = scores(h + 1)
        if pending is not None:
            accumulate(*pending)
        pending = (h,) + probs(h, s_cur)
    accumulate(*pending)

    @pl.when(kv == pl.num_programs(1) - 1)
    def _():
        outs = []
        for h in range(MLA_HEADS):
            a = acc_ref[h]
            outs.append((a / a[MLA_V:MLA_V + 1, :]).T[:, :MLA_V])
        o_ref[...] = jnp.concatenate(outs, axis=1).astype(BF16)


def _attn_call(q, k, vt, n_q, tq, q_blk0, n_k, tk, k_blk0):
    width = MLA_HEADS * HEAD_PAD
    return pl.pallas_call(
        _attn_kernel,
        grid=(n_q // tq, n_k // tk),
        in_specs=[
            pl.BlockSpec((tq, width), lambda i, j: (q_blk0 + i, 0)),
            pl.BlockSpec((tk, width), lambda i, j: (k_blk0 + j, 0)),
            pl.BlockSpec((width, tk), lambda i, j: (0, k_blk0 + j)),
        ],
        out_specs=pl.BlockSpec((tq, MLA_HEADS * MLA_V), lambda i, j: (i, 0)),
        out_shape=jax.ShapeDtypeStruct((n_q, MLA_HEADS * MLA_V), BF16),
        scratch_shapes=[
            pltpu.VMEM((MLA_HEADS, 1, tq), F32),
            pltpu.VMEM((MLA_HEADS, HEAD_PAD, tq), F32),
        ],
        compiler_params=_params(("parallel", "arbitrary")),
        name="attn",
    )(q, k, vt)


def _outproj_kernel(a_ref, b_ref, c_ref, d_ref, x_ref, m_ref, wout_ref, gn_ref, rwt_ref, rb_ref, tri_ref,
                    lat_ref, fin_ref, tope_ref, gatet_ref, rank_ref, cnt_ref,
                    carry_ref):
    @pl.when(pl.program_id(0) == 0)
    def _():
        carry_ref[...] = jnp.zeros_like(carry_ref)

    m = m_ref[0]
    y = jnp.dot(jnp.concatenate([a_ref[...], b_ref[...], c_ref[...], d_ref[...]], axis=1),
                wout_ref[...], preferred_element_type=F32)
    lat = x_ref[...] + m[2:3] * y
    lat_ref[...] = lat
    f = _rms(lat, lat.shape[-1]) * gn_ref[...]
    f = f * (1.0 + m[4:5]) + m[3:4]
    fin_ref[...] = f

    logits = lax.dot_general(rwt_ref[...], f, (((1,), (1,)), ((), ())),
                             precision=HIGHEST, preferred_element_type=F32) + rb_ref[...]
    n_e, tm = logits.shape
    e_iota = lax.broadcasted_iota(jnp.int32, (n_e, tm), 0)
    vals, idxs, hots = [], [], []
    for _ in range(TOP_K):
        mx = jnp.max(logits, axis=0, keepdims=True)
        idx = jnp.min(jnp.where(logits == mx, e_iota, n_e), axis=0, keepdims=True)
        hot = e_iota == idx
        vals.append(mx)
        idxs.append(idx)
        hots.append(hot)
        logits = jnp.where(hot, -jnp.inf, logits)
    ex = [jnp.exp(vv - vals[0]) for vv in vals]
    den = ex[0] + ex[1] + ex[2] + ex[3]
    row8 = lax.broadcasted_iota(jnp.int32, (8, tm), 0)
    row128 = lax.broadcasted_iota(jnp.int32, (LANE, tm), 0)
    tope = jnp.zeros((8, tm), jnp.int32)
    gates = jnp.zeros((LANE, tm), F32)
    for k in range(TOP_K):
        tope = jnp.where(row8 == k, idxs[k], tope)
        gates = jnp.where(row128 == k, ex[k] / den, gates)
    tope_ref[...] = tope
    gatet_ref[...] = gates.T

    onehot = jnp.concatenate([jnp.where(hh, 1.0, 0.0) for hh in hots], axis=0)
    prefix = jnp.dot(onehot.astype(BF16), tri_ref[...], preferred_element_type=F32)
    base = carry_ref[:, 0:1]
    ranks = []
    for k in range(TOP_K):
        hot_f = onehot[k * n_e:(k + 1) * n_e]
        ranks.append(jnp.sum(hot_f * (prefix[k * n_e:(k + 1) * n_e] + base), axis=0, keepdims=True))
        base = base + jnp.sum(hot_f, axis=1, keepdims=True)
    rank = jnp.zeros((8, tm), F32)
    for k in range(TOP_K):
        rank = jnp.where(row8 == k, ranks[k], rank)
    rank_ref[...] = rank.astype(jnp.int32)
    carry_ref[...] = jnp.broadcast_to(base, carry_ref.shape)
    cnt_ref[...] = jnp.broadcast_to(base, cnt_ref.shape)


def _outproj_call(a, b, c, d, xcat, mods, wout, gn, rwt, rb, tri, nt_lat):
    T, D = xcat.shape
    nt = T // TILE
    n_e = rwt.shape[0]
    row = lambda i: (i, 0)
    col = lambda i: (0, i)
    fixed = lambda i: (0, 0)

    def full(arr):
        return pl.BlockSpec(arr.shape, fixed)

    g_spec = pl.BlockSpec((TILE, GROUP_W), row)
    out_shapes = (
        jax.ShapeDtypeStruct((T, D), F32),
        jax.ShapeDtypeStruct((T, D), F32),
        jax.ShapeDtypeStruct((8, T), jnp.int32),
        jax.ShapeDtypeStruct((T, LANE), F32),
        jax.ShapeDtypeStruct((8, T), jnp.int32),
        jax.ShapeDtypeStruct((n_e, LANE), F32),
    )
    out_specs = (
        pl.BlockSpec((TILE, D), row), pl.BlockSpec((TILE, D), row),
        pl.BlockSpec((8, TILE), col),
        pl.BlockSpec((TILE, LANE), row), pl.BlockSpec((8, TILE), col),
        pl.BlockSpec((n_e, LANE), fixed),
    )
    return pl.pallas_call(
        _outproj_kernel,
        grid=(nt,),
        in_specs=[
            g_spec, g_spec, g_spec, g_spec,
            pl.BlockSpec((TILE, D), row),
            pl.BlockSpec((1, 6, D), lambda i: (i // nt_lat, 0, 0)),
            full(wout), full(gn), full(rwt), full(rb), full(tri),
        ],
        out_specs=out_specs,
        out_shape=out_shapes,
        scratch_shapes=[pltpu.VMEM((n_e, LANE), F32)],
        compiler_params=_params(("arbitrary",)),
        name="outproj",
    )(a, b, c, d, xcat, mods, wout, gn, rwt, rb, tri)


def _row_copy(src, s, dst, d, sem):
    return pltpu.make_async_copy(src.at[pl.ds(s, 1)], dst.at[pl.ds(d, 1)], sem)


def _dispatch_kernel(dest_ref, fin_ref, xs_in_ref, xs_ref, sem):
    del xs_in_ref

    def issue(r, carry):
        for k in range(TOP_K):
            _row_copy(fin_ref, r, xs_ref, dest_ref[0, 0, k * TILE + r], sem).start()
        return carry

    lax.fori_loop(0, TILE, issue, 0)

    def drain(r, carry):
        for k in range(TOP_K):
            _row_copy(fin_ref, 0, xs_ref, 0, sem).wait()
        return carry

    lax.fori_loop(0, TILE, drain, 0)


def _dispatch_call(dest3, fin, xs_zero):
    nt = dest3.shape[0]
    return pl.pallas_call(
        _dispatch_kernel,
        grid=(nt,),
        in_specs=[
            pl.BlockSpec((1, 1, TOP_K * TILE), lambda i: (i, 0, 0), memory_space=pltpu.SMEM),
            pl.BlockSpec((TILE, fin.shape[1]), lambda i: (i, 0)),
            pl.BlockSpec(memory_space=pl.ANY),
        ],
        out_specs=pl.BlockSpec(memory_space=pl.ANY),
        out_shape=jax.ShapeDtypeStruct(xs_zero.shape, xs_zero.dtype),
        scratch_shapes=[pltpu.SemaphoreType.DMA(())],
        input_output_aliases={2: 0},
        compiler_params=_params(("arbitrary",)),
        name="dispatch",
    )(dest3, fin, xs_zero)


def _expert_kernel(be_ref, nu_ref, xs_ref, wgu_ref, bgu_ref, wdn_ref, bdn_ref, ys_ref, wgu_bf, wdn_bf):
    i = pl.program_id(0)
    changed = jnp.logical_or(i == 0, be_ref[i] != be_ref[jnp.maximum(i - 1, 0)])

    @pl.when(changed)
    def _():
        wgu_bf[...] = wgu_ref[...].astype(BF16)
        wdn_bf[...] = wdn_ref[...].astype(BF16)

    @pl.when(i < nu_ref[0])
    def _():
        f = wdn_bf.shape[0]
        gu = jnp.dot(xs_ref[...].astype(BF16), wgu_bf[...], preferred_element_type=F32) + bgu_ref[...]
        g = jnp.minimum(gu[:, :f], SWIGLU_LIMIT)
        u = jnp.clip(gu[:, f:], -SWIGLU_LIMIT, SWIGLU_LIMIT)
        act = (u + 1.0) * (g * jax.nn.sigmoid(SWIGLU_ALPHA * g))
        ys_ref[...] = jnp.dot(act.astype(BF16), wdn_bf[...], preferred_element_type=F32) + bdn_ref[...]

    @pl.when(i >= nu_ref[0])
    def _():
        ys_ref[...] = jnp.zeros_like(ys_ref)


def _expert_call(block_e, n_used, xs, w_gu, b_gu, w_down, b_down, layer):
    n_rows, D = xs.shape
    _, E, _, F2 = w_gu.shape
    F = F2 // 2
    grid_spec = pltpu.PrefetchScalarGridSpec(
        num_scalar_prefetch=2,
        grid=(n_rows // MOE_BM,),
        in_specs=[
            pl.BlockSpec((MOE_BM, D), lambda i, be, nu: (i, 0)),
            pl.BlockSpec((None, None, D, F2), lambda i, be, nu: (layer, be[i], 0, 0)),
            pl.BlockSpec((None, None, 1, F2), lambda i, be, nu: (layer, be[i], 0, 0)),
            pl.BlockSpec((None, None, F, D), lambda i, be, nu: (layer, be[i], 0, 0)),
            pl.BlockSpec((None, None, 1, D), lambda i, be, nu: (layer, be[i], 0, 0)),
        ],
        out_specs=pl.BlockSpec((MOE_BM, D), lambda i, be, nu: (i, 0)),
        scratch_shapes=[pltpu.VMEM((D, F2), BF16), pltpu.VMEM((F, D), BF16)],
    )
    L = w_gu.shape[0]
    return pl.pallas_call(
        _expert_kernel,
        grid_spec=grid_spec,
        out_shape=jax.ShapeDtypeStruct((n_rows, D), F32),
        compiler_params=_params(("arbitrary",)),
        name="expert",
    )(block_e, n_used, xs, w_gu, b_gu.reshape(L, E, 1, F2), w_down, b_down.reshape(L, E, 1, D))


def _combine_kernel(dest_ref, ys_ref, gt_ref, lat_ref, m_ref, out_ref, buf_ref, sem):
    def issue(r, carry):
        for k in range(TOP_K):
            _row_copy(ys_ref, dest_ref[0, 0, k * TILE + r], buf_ref.at[k], r, sem).start()
        return carry

    lax.fori_loop(0, TILE, issue, 0)

    def drain(r, carry):
        for k in range(TOP_K):
            _row_copy(ys_ref, 0, buf_ref.at[k], 0, sem).wait()
        return carry

    lax.fori_loop(0, TILE, drain, 0)

    gt = gt_ref[...]
    f = gt[:, 0:1] * buf_ref[0]
    for k in range(1, TOP_K):
        f = f + gt[:, k:k + 1] * buf_ref[k]
    out_ref[...] = lat_ref[...] + m_ref[0][5:6] * f


def _combine_call(dest3, ys, gates_t, lat, mods, nt_lat):
    T, D = lat.shape
    nt = T // TILE
    row = lambda i: (i, 0)
    return pl.pallas_call(
        _combine_kernel,
        grid=(nt,),
        in_specs=[
            pl.BlockSpec((1, 1, TOP_K * TILE), lambda i: (i, 0, 0), memory_space=pltpu.SMEM),
            pl.BlockSpec(memory_space=pl.ANY),
            pl.BlockSpec((TILE, LANE), row),
            pl.BlockSpec((TILE, D), row),
            pl.BlockSpec((1, 6, D), lambda i: (i // nt_lat, 0, 0)),
        ],
        out_specs=pl.BlockSpec((TILE, D), row),
        out_shape=jax.ShapeDtypeStruct((T, D), F32),
        scratch_shapes=[pltpu.VMEM((TOP_K, TILE, D), F32), pltpu.SemaphoreType.DMA(())],
        compiler_params=_params(("arbitrary",)),
        name="combine",
    )(dest3, ys, gates_t, lat, mods)


def _block_diag(blocks):
    H, a, b = blocks.shape
    eye = jnp.eye(H, dtype=blocks.dtype)
    return (eye[:, None, :, None] * blocks[:, :, None, :]).reshape(H * a, H * b)


def _head_slabs(w, width):
    lead = w.shape[:-1]
    w = w.reshape(lead + (MLA_HEADS, width))
    w = jnp.pad(w, [(0, 0)] * len(lead) + [(0, 0), (0, HEAD_PAD - width)])
    return w.reshape(lead + (MLA_HEADS * HEAD_PAD,))


def _rope_tables(rows, n_ctx):
    n_freq = MLA_ROPE // 4
    inv = ROPE_BASE ** (-jnp.arange(n_freq, dtype=F32) / n_freq)
    r = jnp.repeat(jnp.arange(rows, dtype=F32), GRID_W)
    col = jnp.tile(jnp.arange(GRID_W, dtype=F32), rows)
    ang = jnp.concatenate([r[:, None] * inv, col[:, None] * inv], axis=-1)
    cos, sin = jnp.cos(ang), jnp.sin(ang)
    n = cos.shape[0]
    half = MLA_ROPE // 2
    pad = HEAD_PAD - MLA_QK
    c_tab = jnp.concatenate([jnp.ones((n, MLA_NOPE), F32), cos, cos, jnp.ones((n, pad), F32)], axis=1)
    s_tab = jnp.concatenate([jnp.zeros((n, MLA_NOPE), F32), -sin, sin, jnp.zeros((n, pad), F32)], axis=1)
    c_tab = jnp.concatenate([c_tab, jnp.ones((n_ctx, HEAD_PAD), F32)], axis=0)
    s_tab = jnp.concatenate([s_tab, jnp.zeros((n_ctx, HEAD_PAD), F32)], axis=0)
    del half
    return c_tab, s_tab


def kernel(x, c, ctx, c_ctx, w_mod, b_mod, norm_mix, norm_ffn, w_in, w_out, pool_w, pool_scale, mla_q_a_norm, mla_w_uq, mla_kv_a_norm, mla_w_ukv, mla_q_norm, mla_k_norm, sgu_norm_g, sgu_norm_b, sgu_ws, sgu_b, lru_conv_w, lru_conv_b, lru_wa, lru_ba, lru_wx, lru_bx, lru_lambda, router_w, router_b, moe_w_gu, moe_b_gu, moe_w_down, moe_b_down):
    B, N, D = x.shape
    n_ctx = ctx.shape[1]
    assert B == 1 and N % TILE == 0 and n_ctx == TILE
    L = w_mod.shape[0]
    T = N + n_ctx
    nt = T // TILE
    nt_lat = N // TILE

    lat = jnp.concatenate([x[0], ctx[0]], axis=0)
    crows = jnp.zeros((8, D), F32).at[0].set(c[0]).at[1].set(c_ctx)
    mods_all = _mod_call(crows, w_mod, b_mod)
    c_tab, s_tab = _rope_tables(N // GRID_W, n_ctx)
    tri = jnp.triu(jnp.ones((TILE, TILE), F32), k=1).astype(BF16)

    n_assign = T * TOP_K
    n_blocks = -(-(n_assign + N_EXPERTS * (MOE_BM - 1)) // MOE_BM)
    n_rows = n_blocks * MOE_BM

    for l in range(L):
        mods = mods_all[l, :2].reshape(2, 6, D)

        wa_, wqa, wkva, wkr, wc_, wdx, wdg = jnp.split(
            w_in[l], [256, 448, 576, 608, 1120, 1376], axis=1)
        wkr_placed = jnp.pad(wkr.reshape(D, 1, MLA_ROPE),
                             ((0, 0), (0, 0), (MLA_NOPE, HEAD_PAD - MLA_QK)))
        wkr_placed = jnp.tile(wkr_placed, (1, MLA_HEADS, 1)).reshape(D, MLA_HEADS * HEAD_PAD)
        win = jnp.concatenate(
            [wa_, jnp.pad(wqa, ((0, 0), (0, 256 - MLA_Q_RANK))), wkva, wkr_placed, wc_, wdx, wdg],
            axis=1).astype(BF16)
        assert win.shape[1] == IN_COLS_P
        gqa = jnp.pad(mla_q_a_norm[l], (0, 256 - MLA_Q_RANK)).reshape(1, 256)
        wuq = jnp.pad(_head_slabs(mla_w_uq[l], MLA_QK), ((0, 256 - MLA_Q_RANK), (0, 0))).astype(BF16)
        gkva = mla_kv_a_norm[l].reshape(1, MLA_KV_RANK)
        wukv3 = mla_w_ukv[l].reshape(MLA_KV_RANK, MLA_HEADS, MLA_NOPE + MLA_V)
        wuk = _head_slabs(wukv3[:, :, :MLA_NOPE].reshape(MLA_KV_RANK, -1), MLA_NOPE)
        wuv = _head_slabs(wukv3[:, :, MLA_NOPE:].reshape(MLA_KV_RANK, -1), MLA_V)
        wukv = jnp.concatenate([wuk, wuv], axis=1).astype(BF16)
        gq = jnp.pad(mla_q_norm[l], (0, HEAD_PAD - MLA_QK)).reshape(1, HEAD_PAD)
        gk = jnp.pad(mla_k_norm[l], (0, HEAD_PAD - MLA_QK)).reshape(1, HEAD_PAD)
        wscat = sgu_ws[l].transpose(1, 0, 2).reshape(SGU_CHUNK, SGU_HEADS * SGU_CHUNK).astype(BF16)
        sbias = jnp.repeat(sgu_b[l].T, SGU_HD, axis=1)

        pa, q, k, vt, c_s, xd, gd = _inproj_call(
            lat, mods, norm_mix[l].reshape(1, D), win, gqa, wuq, gkva, wukv, gq, gk, c_tab, s_tab,
            sgu_norm_g[l].reshape(1, GROUP_W), sgu_norm_b[l].reshape(1, GROUP_W), wscat, sbias, nt_lat)

        wpool = _block_diag(pool_w[l]).astype(BF16)
        wlru = jnp.concatenate(
            [_block_diag(lru_wa[l, 0]), _block_diag(lru_wx[l, 0]),
             _block_diag(lru_wa[l, 1]), _block_diag(lru_wx[l, 1])], axis=1).astype(BF16)
        blru = jnp.concatenate([lru_ba[l, 0], lru_bx[l, 0], lru_ba[l, 1], lru_bx[l, 1]]).reshape(1, -1)
        a_s, hf, ab, bb = _seq_fwd_call(
            pa, xd, wpool, pool_scale[l].reshape(1, GROUP_W), lru_conv_w[l],
            lru_conv_b[l].reshape(1, GROUP_W), wlru, blru, lru_lambda[l], nt_lat, N, n_ctx)
        d_s = _seq_bwd_call(ab, bb, hf, gd)

        tk = next(t for t in (3328, 1280, TILE) if T % t == 0)
        b_lat = _attn_call(q, k, vt, N, 512, 0, T, tk, 0)
        if l < L - 1:
            b_ctx = _attn_call(q, k, vt, n_ctx, n_ctx, N // n_ctx, n_ctx, n_ctx, N // n_ctx)
        else:
            b_ctx = jnp.zeros((n_ctx, MLA_HEADS * MLA_V), BF16)
        b_s = jnp.concatenate([b_lat, b_ctx], axis=0)

        lat2, fin, top_e, gates_t, rank, cnt = _outproj_call(
            a_s, b_s, c_s, d_s, lat, mods, w_out[l].astype(BF16), norm_ffn[l].reshape(1, D),
            router_w[l].T, router_b[l].reshape(N_EXPERTS, 1), tri, nt_lat)

        counts = cnt[:, 0].astype(jnp.int32)
        padded = (counts + MOE_BM - 1) // MOE_BM * MOE_BM
        pad_end = jnp.cumsum(padded)
        pad_start = pad_end - padded
        e_ids = jnp.arange(N_EXPERTS, dtype=jnp.int32)
        start_of = jnp.sum(jnp.where(top_e[None, :TOP_K] == e_ids[:, None, None],
                                     pad_start[:, None, None], 0), axis=0)
        dest = start_of + rank[:TOP_K]
        dest3 = dest.reshape(TOP_K, nt, TILE).transpose(1, 0, 2).reshape(nt, 1, TOP_K * TILE)
        block_row0 = jnp.arange(n_blocks, dtype=jnp.int32) * MOE_BM
        block_e = jnp.minimum(
            jnp.sum((pad_end[None, :] <= block_row0[:, None]).astype(jnp.int32), axis=1),
            N_EXPERTS - 1)
        n_used = (pad_end[-1:] // MOE_BM).astype(jnp.int32)

        xs = _dispatch_call(dest3, fin, jnp.zeros((n_rows, D), F32))
        ys = _expert_call(block_e, n_used, xs, moe_w_gu, moe_b_gu, moe_w_down, moe_b_down, l)
        lat = _combine_call(dest3, ys, gates_t, lat2, mods, nt_lat)

    return lat[:N].reshape(B, N, D)
```

```python
import functools

import jax
import jax.numpy as jnp
from jax import lax
from jax.experimental import pallas as pl
from jax.experimental.pallas import tpu as pltpu

F32 = jnp.float32
BF16 = jnp.bfloat16
HIGHEST = lax.Precision.HIGHEST

EPS = 1e-6
LOG2_E = 1.4426950408889634
GRID_W = 64
GROUP_W = 256
POOL_WINDOWS = (2, 4, 8, 16)
POOL_CH = GROUP_W // len(POOL_WINDOWS)
MLA_HEADS = 4
MLA_NOPE = 64
MLA_ROPE = 32
MLA_QK = MLA_NOPE + MLA_ROPE
MLA_V = 64
MLA_Q_RANK = 192
MLA_KV_RANK = 128
ROPE_BASE = 10000.0
SGU_HEADS = 4
SGU_HD = GROUP_W // SGU_HEADS
SGU_CHUNK = 128
LRU_HEADS = 4
LRU_HD = GROUP_W // LRU_HEADS
CONV_W = 4
CONV_LEFT = 1
LRU_C = 8.0
N_EXPERTS = 32
TOP_K = 4
SWIGLU_LIMIT = 7.0
SWIGLU_ALPHA = 1.702

LANE = 128
HEAD_PAD = LANE
TILE = 256
TOK_TILE = 640
HALO = 16
ATTN_CHUNK = 256
MOE_BM = 256
VMEM_LIMIT = 56 * 1024 * 1024

C_A = 0
C_QA = C_A + GROUP_W
C_KVA = C_QA + 256
C_KR = C_KVA + MLA_KV_RANK
C_CU = C_KR + MLA_HEADS * HEAD_PAD
C_CV = C_CU + GROUP_W
C_DX = C_CV + GROUP_W
C_DG = C_DX + GROUP_W
IN_COLS_P = C_DG + GROUP_W


def _params(sem, vmem=VMEM_LIMIT):
    return pltpu.CompilerParams(dimension_semantics=sem, vmem_limit_bytes=vmem)


def _rms(x, n):
    return x * lax.rsqrt(jnp.sum(x * x, axis=-1, keepdims=True) * (1.0 / n) + EPS)


def _mod_kernel(c_ref, w_ref, b_ref, o_ref):
    cv = c_ref[...]
    s = cv * jax.nn.sigmoid(cv)
    o_ref[0] = jnp.dot(s, w_ref[0], precision=HIGHEST, preferred_element_type=F32) + b_ref[0]


def _mod_call(crows, w_mod, b_mod):
    L, D, M = w_mod.shape
    bn = 1536
    return pl.pallas_call(
        _mod_kernel,
        grid=(L, M // bn),
        in_specs=[
            pl.BlockSpec((8, D), lambda l, j: (0, 0)),
            pl.BlockSpec((1, D, bn), lambda l, j: (l, 0, j)),
            pl.BlockSpec((1, 1, bn), lambda l, j: (l, 0, j)),
        ],
        out_specs=pl.BlockSpec((1, 8, bn), lambda l, j: (l, 0, j)),
        out_shape=jax.ShapeDtypeStruct((L, 8, M), F32),
        compiler_params=_params(("arbitrary", "arbitrary")),
        name="mod",
    )(crows, w_mod, b_mod.reshape(L, 1, M))


def _rope_heads(x, gain, c_tab, s_tab, lane, scale):
    outs = []
    for h in range(MLA_HEADS):
        xh = x[:, h * HEAD_PAD:(h + 1) * HEAD_PAD]
        xn = _rms(xh, MLA_QK) * gain
        swap = jnp.where(lane < MLA_NOPE + MLA_ROPE // 2,
                         pltpu.roll(xn, HEAD_PAD - MLA_ROPE // 2, axis=1),
                         pltpu.roll(xn, MLA_ROPE // 2, axis=1))
        r = xn * c_tab + swap * s_tab
        if scale != 1.0:
            r = r * scale
        outs.append(r.astype(BF16))
    return jnp.concatenate(outs, axis=1)


def _mod_row(m_ref, is_ctx, k):
    return jnp.where(is_ctx, m_ref[1, k:k + 1, :], m_ref[0, k:k + 1, :])


def _is_ctx_rows(tm, n_lat):
    row = pl.program_id(0) * tm + lax.broadcasted_iota(jnp.int32, (tm, 1), 0)
    return row >= n_lat


def _inproj_kernel(n_lat, x_ref, m_ref, gn_ref, win_ref, gqa_ref, wuq_ref, gkva_ref, wukv_ref,
                   gq_ref, gk_ref, ct_ref, st_ref, sg_ref, sb_ref, ws_ref, sbias_ref,
                   pa_ref, q_ref, k_ref, vt_ref, c_ref, xd_ref, gd_ref):
    x = x_ref[...]
    is_ctx = _is_ctx_rows(x.shape[0], n_lat)
    h = _rms(x, x.shape[-1]) * gn_ref[...]
    h = h * (1.0 + _mod_row(m_ref, is_ctx, 1)) + _mod_row(m_ref, is_ctx, 0)
    p = jnp.dot(h.astype(BF16), win_ref[...], preferred_element_type=F32)

    pa_ref[...] = p[:, C_A:C_A + GROUP_W]
    xd_ref[...] = p[:, C_DX:C_DX + GROUP_W]
    gd_ref[...] = jax.nn.gelu(p[:, C_DG:C_DG + GROUP_W])

    lane = lax.broadcasted_iota(jnp.int32, (x.shape[0], HEAD_PAD), 1)
    c_tab = ct_ref[...]
    s_tab = st_ref[...]
    qa = _rms(p[:, C_QA:C_QA + 256], MLA_Q_RANK) * gqa_ref[...]
    q = jnp.dot(qa.astype(BF16), wuq_ref[...], preferred_element_type=F32)
    q_ref[...] = _rope_heads(q, gq_ref[...], c_tab, s_tab, lane, MLA_QK ** -0.5 * LOG2_E)

    kva = _rms(p[:, C_KVA:C_KVA + MLA_KV_RANK], MLA_KV_RANK) * gkva_ref[...]
    kv = jnp.dot(kva.astype(BF16), wukv_ref[...], preferred_element_type=F32)
    kpre = kv[:, :MLA_HEADS * HEAD_PAD] + p[:, C_KR:C_KR + MLA_HEADS * HEAD_PAD]
    k_ref[...] = _rope_heads(kpre, gk_ref[...], c_tab, s_tab, lane, 1.0)
    v = kv[:, MLA_HEADS * HEAD_PAD:]
    slab_lane = lax.broadcasted_iota(jnp.int32, v.shape, 1) % HEAD_PAD
    vt_ref[...] = jnp.where(slab_lane == MLA_V, 1.0, v).T.astype(BF16)

    z = jax.nn.gelu(p[:, C_CU:C_CU + 2 * GROUP_W])
    u = z[:, :GROUP_W]
    v = z[:, GROUP_W:]
    mu = jnp.mean(v, axis=-1, keepdims=True)
    vc = v - mu
    var = jnp.mean(vc * vc, axis=-1, keepdims=True)
    vn = (vc * lax.rsqrt(var + EPS) * sg_ref[...] + sb_ref[...]).astype(BF16)
    head_of_lane = lax.broadcasted_iota(jnp.int32, (SGU_CHUNK, GROUP_W), 1) // SGU_HD
    zero = jnp.zeros((SGU_CHUNK, GROUP_W), BF16)
    for cch in range(x.shape[0] // SGU_CHUNK):
        rows = slice(cch * SGU_CHUNK, (cch + 1) * SGU_CHUNK)
        vch = vn[rows]
        stacked = jnp.concatenate(
            [jnp.where(head_of_lane == hh, vch, zero) for hh in range(SGU_HEADS)], axis=0)
        mixed = jnp.dot(ws_ref[...], stacked, preferred_element_type=F32) + sbias_ref[...]
        c_ref[rows, :] = (u[rows] * mixed).astype(BF16)


def _inproj_call(xcat, mods, gn, win, gqa, wuq, gkva, wukv, gq, gk, ctab, stab, sg, sb, wscat, sbias,
                 n_lat):
    T, D = xcat.shape
    tm = TILE
    nt = T // tm
    row = lambda i: (i, 0)

    def full(a):
        return pl.BlockSpec(a.shape, lambda i: (0,) * a.ndim)

    out_shapes = (
        jax.ShapeDtypeStruct((T, GROUP_W), F32),
        jax.ShapeDtypeStruct((T, MLA_HEADS * HEAD_PAD), BF16),
        jax.ShapeDtypeStruct((T, MLA_HEADS * HEAD_PAD), BF16),
        jax.ShapeDtypeStruct((MLA_HEADS * HEAD_PAD, T), BF16),
        jax.ShapeDtypeStruct((T, GROUP_W), BF16),
        jax.ShapeDtypeStruct((T, GROUP_W), F32),
        jax.ShapeDtypeStruct((T, GROUP_W), F32),
    )
    out_specs = [pl.BlockSpec((tm, s.shape[1]), row) for s in out_shapes]
    out_specs[3] = pl.BlockSpec((MLA_HEADS * HEAD_PAD, tm), lambda i: (0, i))
    return pl.pallas_call(
        functools.partial(_inproj_kernel, n_lat),
        grid=(nt,),
        in_specs=[
            pl.BlockSpec((tm, D), row),
            full(mods),
            full(gn), full(win), full(gqa), full(wuq), full(gkva), full(wukv), full(gq), full(gk),
            pl.BlockSpec((tm, HEAD_PAD), row),
            pl.BlockSpec((tm, HEAD_PAD), row),
            full(sg), full(sb), full(wscat), full(sbias),
        ],
        out_specs=tuple(out_specs),
        out_shape=out_shapes,
        compiler_params=_params(("parallel",)),
        name="inproj",
    )(xcat, mods, gn, win, gqa, wuq, gkva, wukv, gq, gk, ctab, stab, sg, sb, wscat, sbias)


def _shift_rows(x, d, fill, reverse):
    n = x.shape[0]
    t = lax.broadcasted_iota(jnp.int32, x.shape, 0)
    if reverse:
        return jnp.where(t < n - d, pltpu.roll(x, n - d, axis=0), fill)
    return jnp.where(t >= d, pltpu.roll(x, d, axis=0), fill)


def _tile_scan(a, b, reverse):
    d = 1
    while d < a.shape[0]:
        a_s = _shift_rows(a, d, 1.0, reverse)
        b_s = _shift_rows(b, d, 0.0, reverse)
        b = a * b_s + b
        a = a * a_s
        d *= 2
    return a, b


def _lru_coeffs(xc, proj, sp, d):
    r = jax.nn.sigmoid(proj[:, (2 * d) * GROUP_W:(2 * d + 1) * GROUP_W])
    i = jax.nn.sigmoid(proj[:, (2 * d + 1) * GROUP_W:(2 * d + 2) * GROUP_W])
    log_a = -LRU_C * r * sp[d:d + 1]
    a = jnp.exp(log_a)
    drive = jnp.sqrt(1.0 - jnp.exp(2.0 * log_a)) * (i * xc)
    return a, drive


def _seq_fwd_kernel(nt_lat, n_lat, n_ctx,
                    pa_ref, pa_prev_ref, pa_next_ref, xd_ref, xd_prev_ref, xd_next_ref,
                    wpool_ref, pscale_ref, cw_ref, cb_ref, wlru_ref, blru_ref, lam_ref,
                    a_out_ref, hf_ref, ab_ref, bb_ref,
                    ext_ref, carry_ref):
    j = pl.program_id(0)
    nt = pl.num_programs(0)
    ti = (j + nt_lat) % nt
    is_ctx = ti >= nt_lat
    seq_first = jnp.logical_or(ti == 0, ti == nt_lat)
    seq_last = jnp.logical_or(ti == nt_lat - 1, ti == nt - 1)
    t_loc = (ti - jnp.where(is_ctx, nt_lat, 0)) * TILE
    n_seq = jnp.where(is_ctx, n_ctx, n_lat)

    @pl.when(j == 0)
    def _():
        carry_ref[...] = jnp.zeros_like(carry_ref)

    def load_ext(cur_ref, prev_ref, next_ref):
        ext_ref[0:HALO, :] = jnp.where(seq_first, 0.0, prev_ref[...])
        ext_ref[HALO:HALO + TILE, :] = cur_ref[...]
        ext_ref[HALO + TILE:, :] = jnp.where(seq_last, 0.0, next_ref[...])

    def win(off):
        return ext_ref[pl.ds(HALO + off, TILE), :]

    load_ext(pa_ref, pa_prev_ref, pa_next_ref)
    x = pa_ref[...]
    t = t_loc + lax.broadcasted_iota(jnp.int32, (TILE, GROUP_W), 0)
    lane = lax.broadcasted_iota(jnp.int32, (TILE, GROUP_W), 1)
    mean = jnp.zeros((TILE, GROUP_W), F32)
    for g, w in enumerate(POOL_WINDOWS):
        acc = win(-(w // 2))
        for off in range(-(w // 2) + 1, w // 2):
            acc = acc + win(off)
        cnt = jnp.minimum(t + w // 2, n_seq) - jnp.maximum(t - w // 2, 0)
        mean = jnp.where(lane // POOL_CH == g, acc / cnt.astype(F32), mean)
    diff = (mean - x).astype(BF16)
    pooled = jnp.dot(diff, wpool_ref[...], preferred_element_type=F32) * pscale_ref[...]
    a_out_ref[...] = pooled.astype(BF16)

    load_ext(xd_ref, xd_prev_ref, xd_next_ref)
    xc = jnp.zeros((TILE, GROUP_W), F32) + cb_ref[...]
    for k in range(CONV_W):
        xc = xc + win(k - CONV_LEFT) * cw_ref[k:k + 1, :]

    proj = jnp.dot(xc.astype(BF16), wlru_ref[...], preferred_element_type=F32) + blru_ref[...]
    lam = lam_ref[...]
    sp = jnp.maximum(-lam, 0.0) + jnp.log(1.0 + jnp.exp(-jnp.abs(lam)))
    a_f, b_f = _lru_coeffs(xc, proj, sp, 0)
    a_b, b_b = _lru_coeffs(xc, proj, sp, 1)
    ab_ref[...] = a_b
    bb_ref[...] = b_b

    big_a, big_b = _tile_scan(a_f, b_f, reverse=False)
    hf = big_b + big_a * carry_ref[0:1, :]
    hf_ref[...] = hf
    carry_ref[0:1, :] = hf[TILE - 1:TILE, :]


def _seq_fwd_call(pa, xd, wpool, pscale, cw, cb, wlru, blru, lam, nt_lat, n_lat, n_ctx):
    T = pa.shape[0]
    nt = T // TILE
    hb = TILE // HALO
    n_halo = T // HALO

    def tile_of(j):
        return (j + nt_lat) % nt

    cur = lambda j: (tile_of(j), 0)
    prev = lambda j: (jnp.maximum(tile_of(j) * hb - 1, 0), 0)
    nxt = lambda j: (jnp.minimum((tile_of(j) + 1) * hb, n_halo - 1), 0)
    fixed = lambda j: (0, 0)

    def full(a):
        return pl.BlockSpec(a.shape, fixed)

    tile_spec = pl.BlockSpec((TILE, GROUP_W), cur)
    out_shapes = (
        jax.ShapeDtypeStruct((T, GROUP_W), BF16),
        jax.ShapeDtypeStruct((T, GROUP_W), F32),
        jax.ShapeDtypeStruct((T, GROUP_W), F32),
        jax.ShapeDtypeStruct((T, GROUP_W), F32),
    )
    return pl.pallas_call(
        functools.partial(_seq_fwd_kernel, nt_lat, n_lat, n_ctx),
        grid=(nt,),
        in_specs=[
            tile_spec, pl.BlockSpec((HALO, GROUP_W), prev), pl.BlockSpec((HALO, GROUP_W), nxt),
            tile_spec, pl.BlockSpec((HALO, GROUP_W), prev), pl.BlockSpec((HALO, GROUP_W), nxt),
            full(wpool), full(pscale), full(cw), full(cb), full(wlru), full(blru), full(lam),
        ],
        out_specs=tuple(pl.BlockSpec((TILE, GROUP_W), cur) for _ in out_shapes),
        out_shape=out_shapes,
        scratch_shapes=[pltpu.VMEM((TILE + 2 * HALO, GROUP_W), F32), pltpu.VMEM((8, GROUP_W), F32)],
        compiler_params=_params(("arbitrary",)),
        name="seq_fwd",
    )(pa, pa, pa, xd, xd, xd, wpool, pscale, cw, cb, wlru, blru, lam)


def _seq_bwd_kernel(ab_ref, bb_ref, hf_ref, gd_ref, d_out_ref, carry_ref):
    @pl.when(pl.program_id(0) == 0)
    def _():
        carry_ref[...] = jnp.zeros_like(carry_ref)

    big_a, big_b = _tile_scan(ab_ref[...], bb_ref[...], reverse=True)
    hb = big_b + big_a * carry_ref[0:1, :]
    carry_ref[0:1, :] = hb[0:1, :]
    d_out_ref[...] = (gd_ref[...] * (hf_ref[...] + hb)).astype(BF16)


def _seq_bwd_call(ab, bb, hf, gd):
    T = ab.shape[0]
    nt = T // TILE
    spec = pl.BlockSpec((TILE, GROUP_W), lambda j: (nt - 1 - j, 0))
    return pl.pallas_call(
        _seq_bwd_kernel,
        grid=(nt,),
        in_specs=[spec, spec, spec, spec],
        out_specs=spec,
        out_shape=jax.ShapeDtypeStruct((T, GROUP_W), BF16),
        scratch_shapes=[pltpu.VMEM((8, GROUP_W), F32)],
        compiler_params=_params(("arbitrary",)),
        name="seq_bwd",
    )(ab, bb, hf, gd)


def _attn_kernel(q_ref, k_ref, vt_ref, o_ref, m_ref, acc_ref):
    kv = pl.program_id(1)

    @pl.when(kv == 0)
    def _():
        m_ref[...] = jnp.full_like(m_ref, -jnp.inf)
        acc_ref[...] = jnp.zeros_like(acc_ref)

    def head_cols(h):
        return slice(h * HEAD_PAD, (h + 1) * HEAD_PAD)

    ck = ATTN_CHUNK if k_ref.shape[0] % ATTN_CHUNK == 0 else k_ref.shape[0]
    items = [(c, h) for c in range(k_ref.shape[0] // ck) for h in range(MLA_HEADS)]

    def scores(c, h):
        cols = head_cols(h)
        return lax.dot_general(k_ref[c * ck:(c + 1) * ck, cols], q_ref[:, cols], (((1,), (1,)), ((), ())),
                               preferred_element_type=F32)

    def probs(h, s):
        m_prev = m_ref[h]
        m_new = jnp.maximum(m_prev, jnp.max(s, axis=0, keepdims=True))
        m_ref[h] = m_new
        return jnp.exp2(m_prev - m_new), jnp.exp2(s - m_new).astype(BF16)

    def accumulate(c, h, alpha, p):
        acc_ref[h] = alpha * acc_ref[h] + jnp.dot(vt_ref[head_cols(h), c * ck:(c + 1) * ck], p,
                                                  preferred_element_type=F32)

    s_next = scores(*items[0])
    pending = None
    for n, (c, h) in enumerate(items):
        s_cur = s_next
        if n + 1 < len(items):
            s_next = scores(*items[n + 1])
        if pending is not None:
            accumulate(*pending)
        pending = (c, h) + probs(h, s_cur)
    accumulate(*pending)

    @pl.when(kv == pl.num_programs(1) - 1)
    def _():
        outs = []
        for h in range(MLA_HEADS):
            a = acc_ref[h]
            outs.append((a / a[MLA_V:MLA_V + 1, :]).T[:, :MLA_V])
        o_ref[...] = jnp.concatenate(outs, axis=1).astype(BF16)


def _attn_call(q, k, vt, n_q, tq, q_blk0, n_k, tk, k_blk0):
    width = MLA_HEADS * HEAD_PAD
    return pl.pallas_call(
        _attn_kernel,
        grid=(n_q // tq, n_k // tk),
        in_specs=[
            pl.BlockSpec((tq, width), lambda i, j: (q_blk0 + i, 0)),
            pl.BlockSpec((tk, width), lambda i, j: (k_blk0 + j, 0)),
            pl.BlockSpec((width, tk), lambda i, j: (0, k_blk0 + j)),
        ],
        out_specs=pl.BlockSpec((tq, MLA_HEADS * MLA_V), lambda i, j: (i, 0)),
        out_shape=jax.ShapeDtypeStruct((n_q, MLA_HEADS * MLA_V), BF16),
        scratch_shapes=[
            pltpu.VMEM((MLA_HEADS, 1, tq), F32),
            pltpu.VMEM((MLA_HEADS, HEAD_PAD, tq), F32),
        ],
        compiler_params=_params(("parallel", "arbitrary")),
        name="attn",
    )(q, k, vt)


def _outproj_kernel(n_lat, a_ref, b_ref, c_ref, d_ref, x_ref, m_ref, wout_ref, gn_ref, rw_ref, rb_ref, tri_ref,
                    lat_ref, fin_ref, tope_ref, gatet_ref, rank_ref, cnt_ref,
                    carry_ref):
    @pl.when(pl.program_id(0) == 0)
    def _():
        carry_ref[...] = jnp.zeros_like(carry_ref)

    tm = x_ref.shape[0]
    n_e = rb_ref.shape[0]
    is_ctx = _is_ctx_rows(tm, n_lat)
    y = jnp.dot(jnp.concatenate([a_ref[...], b_ref[...], c_ref[...], d_ref[...]], axis=1),
                wout_ref[...], preferred_element_type=F32)
    lat = x_ref[...] + _mod_row(m_ref, is_ctx, 2) * y
    lat_ref[...] = lat
    f = _rms(lat, lat.shape[-1]) * gn_ref[...]
    f = f * (1.0 + _mod_row(m_ref, is_ctx, 4)) + _mod_row(m_ref, is_ctx, 3)
    fin_ref[...] = f

    f_hi = f.astype(BF16)
    f_lo = (f - f_hi.astype(F32)).astype(BF16)
    hi_prod = jnp.dot(f_hi, rw_ref[...], preferred_element_type=F32)
    lo_prod = jnp.dot(f_lo, rw_ref[:, :LANE], preferred_element_type=F32)
    logits_rows = hi_prod[:, :LANE] + hi_prod[:, LANE:] + lo_prod
    logits = logits_rows.T[:n_e] + rb_ref[...]
    e_iota = lax.broadcasted_iota(jnp.int32, (n_e, tm), 0)
    vals, idxs, hots = [], [], []
    for _ in range(TOP_K):
        mx = jnp.max(logits, axis=0, keepdims=True)
        idx = jnp.min(jnp.where(logits == mx, e_iota, n_e), axis=0, keepdims=True)
        hot = e_iota == idx
        vals.append(mx)
        idxs.append(idx)
        hots.append(hot)
        logits = jnp.where(hot, -jnp.inf, logits)
    ex = [jnp.exp(vv - vals[0]) for vv in vals]
    den = ex[0] + ex[1] + ex[2] + ex[3]
    row8 = lax.broadcasted_iota(jnp.int32, (8, tm), 0)
    row128 = lax.broadcasted_iota(jnp.int32, (LANE, tm), 0)
    tope = jnp.zeros((8, tm), jnp.int32)
    gates = jnp.zeros((LANE, tm), F32)
    for k in range(TOP_K):
        tope = jnp.where(row8 == k, idxs[k], tope)
        gates = jnp.where(row128 == k, ex[k] / den, gates)
    tope_ref[...] = tope
    gatet_ref[...] = gates.T

    onehot = jnp.concatenate([jnp.where(hh, 1.0, 0.0) for hh in hots], axis=0)
    prefix = jnp.dot(onehot.astype(BF16), tri_ref[...], preferred_element_type=F32)
    base = carry_ref[:, 0:1]
    ranks = []
    for k in range(TOP_K):
        hot_f = onehot[k * n_e:(k + 1) * n_e]
        ranks.append(jnp.sum(hot_f * (prefix[k * n_e:(k + 1) * n_e] + base), axis=0, keepdims=True))
        base = base + jnp.sum(hot_f, axis=1, keepdims=True)
    rank = jnp.zeros((8, tm), F32)
    for k in range(TOP_K):
        rank = jnp.where(row8 == k, ranks[k], rank)
    rank_ref[...] = rank.astype(jnp.int32)
    carry_ref[...] = jnp.broadcast_to(base, carry_ref.shape)
    cnt_ref[...] = jnp.broadcast_to(base, cnt_ref.shape)


def _outproj_call(a, b, c, d, xcat, mods, wout, gn, rw, rb, tri, n_lat):
    T, D = xcat.shape
    tm = tri.shape[0]
    n_e = rb.shape[0]
    row = lambda i: (i, 0)
    col = lambda i: (0, i)

    def full(arr):
        return pl.BlockSpec(arr.shape, lambda i: (0,) * arr.ndim)

    g_spec = pl.BlockSpec((tm, GROUP_W), row)
    out_shapes = (
        jax.ShapeDtypeStruct((T, D), F32),
        jax.ShapeDtypeStruct((T, D), F32),
        jax.ShapeDtypeStruct((8, T), jnp.int32),
        jax.ShapeDtypeStruct((T, LANE), F32),
        jax.ShapeDtypeStruct((8, T), jnp.int32),
        jax.ShapeDtypeStruct((n_e, LANE), F32),
    )
    out_specs = (
        pl.BlockSpec((tm, D), row), pl.BlockSpec((tm, D), row),
        pl.BlockSpec((8, tm), col),
        pl.BlockSpec((tm, LANE), row), pl.BlockSpec((8, tm), col),
        pl.BlockSpec((n_e, LANE), lambda i: (0, 0)),
    )
    return pl.pallas_call(
        functools.partial(_outproj_kernel, n_lat),
        grid=(T // tm,),
        in_specs=[
            g_spec, g_spec, g_spec, g_spec,
            pl.BlockSpec((tm, D), row),
            full(mods), full(wout), full(gn), full(rw), full(rb), full(tri),
        ],
        out_specs=out_specs,
        out_shape=out_shapes,
        scratch_shapes=[pltpu.VMEM((n_e, LANE), F32)],
        compiler_params=_params(("arbitrary",)),
        name="outproj",
    )(a, b, c, d, xcat, mods, wout, gn, rw, rb, tri)


def _row_copy(src, s, dst, d, sem):
    return pltpu.make_async_copy(src.at[pl.ds(s, 1)], dst.at[pl.ds(d, 1)], sem)


def _dispatch_kernel(lo_ref, hi_ref, dest_ref, fin_ref, xs_ref, zero_ref, sem, zero_sem):
    @pl.when(pl.program_id(0) == 0)
    def _():
        zero_ref[...] = jnp.zeros_like(zero_ref)

        def per_expert(e, carry):
            lo, hi = lo_ref[e], hi_ref[e]

            def start(r, c):
                _row_copy(zero_ref, 0, xs_ref, r, zero_sem).start()
                return c

            def wait(r, c):
                _row_copy(zero_ref, 0, xs_ref, 0, zero_sem).wait()
                return c

            lax.fori_loop(lo, hi, start, 0)
            lax.fori_loop(lo, hi, wait, 0)
            return carry

        lax.fori_loop(0, lo_ref.shape[0], per_expert, 0)

    def issue(r, carry):
        for k in range(TOP_K):
            _row_copy(fin_ref, r, xs_ref, dest_ref[0, 0, k * TILE + r], sem).start()
        return carry

    lax.fori_loop(0, TILE, issue, 0)

    def drain(r, carry):
        for k in range(TOP_K):
            _row_copy(fin_ref, 0, xs_ref, 0, sem).wait()
        return carry

    lax.fori_loop(0, TILE, drain, 0)


def _dispatch_call(pad_lo, pad_hi, dest3, fin, n_rows):
    nt = dest3.shape[0]
    D = fin.shape[1]
    grid_spec = pltpu.PrefetchScalarGridSpec(
        num_scalar_prefetch=2,
        grid=(nt,),
        in_specs=[
            pl.BlockSpec((1, 1, TOP_K * TILE), lambda i, lo, hi: (i, 0, 0), memory_space=pltpu.SMEM),
            pl.BlockSpec((TILE, D), lambda i, lo, hi: (i, 0)),
        ],
        out_specs=pl.BlockSpec(memory_space=pl.ANY),
        scratch_shapes=[pltpu.VMEM((8, D), F32), pltpu.SemaphoreType.DMA(()), pltpu.SemaphoreType.DMA(())],
    )
    return pl.pallas_call(
        _dispatch_kernel,
        grid_spec=grid_spec,
        out_shape=jax.ShapeDtypeStruct((n_rows, D), F32),
        compiler_params=_params(("arbitrary",)),
        name="dispatch",
    )(pad_lo, pad_hi, dest3, fin)


def _expert_kernel(be_ref, nu_ref, xs_ref, wgu_ref, bgu_ref, wdn_ref, bdn_ref, ys_ref, wgu_bf, wdn_bf):
    i = pl.program_id(0)
    changed = jnp.logical_or(i == 0, be_ref[i] != be_ref[jnp.maximum(i - 1, 0)])

    @pl.when(changed)
    def _():
        wgu_bf[...] = wgu_ref[...].astype(BF16)
        wdn_bf[...] = wdn_ref[...].astype(BF16)

    @pl.when(i < nu_ref[0])
    def _():
        f = wdn_bf.shape[0]
        gu = jnp.dot(xs_ref[...].astype(BF16), wgu_bf[...], preferred_element_type=F32) + bgu_ref[...]
        g = jnp.minimum(gu[:, :f], SWIGLU_LIMIT)
        u = jnp.clip(gu[:, f:], -SWIGLU_LIMIT, SWIGLU_LIMIT)
        act = (u + 1.0) * (g * jax.nn.sigmoid(SWIGLU_ALPHA * g))
        ys_ref[...] = jnp.dot(act.astype(BF16), wdn_bf[...], preferred_element_type=F32) + bdn_ref[...]

    @pl.when(i >= nu_ref[0])
    def _():
        ys_ref[...] = jnp.zeros_like(ys_ref)


def _expert_call(block_e, n_used, xs, w_gu, b_gu, w_down, b_down, layer):
    n_rows, D = xs.shape
    _, E, _, F2 = w_gu.shape
    F = F2 // 2
    grid_spec = pltpu.PrefetchScalarGridSpec(
        num_scalar_prefetch=2,
        grid=(n_rows // MOE_BM,),
        in_specs=[
            pl.BlockSpec((MOE_BM, D), lambda i, be, nu: (jnp.minimum(i, nu[0] - 1), 0)),
            pl.BlockSpec((None, None, D, F2), lambda i, be, nu: (layer, be[i], 0, 0)),
            pl.BlockSpec((None, None, 1, F2), lambda i, be, nu: (layer, be[i], 0, 0)),
            pl.BlockSpec((None, None, F, D), lambda i, be, nu: (layer, be[i], 0, 0)),
            pl.BlockSpec((None, None, 1, D), lambda i, be, nu: (layer, be[i], 0, 0)),
        ],
        out_specs=pl.BlockSpec((MOE_BM, D), lambda i, be, nu: (i, 0)),
        scratch_shapes=[pltpu.VMEM((D, F2), BF16), pltpu.VMEM((F, D), BF16)],
    )
    L = w_gu.shape[0]
    return pl.pallas_call(
        _expert_kernel,
        grid_spec=grid_spec,
        out_shape=jax.ShapeDtypeStruct((n_rows, D), F32),
        compiler_params=_params(("arbitrary",)),
        name="expert",
    )(block_e, n_used, xs, w_gu, b_gu.reshape(L, E, 1, F2), w_down, b_down.reshape(L, E, 1, D))


def _combine_kernel(dest_ref, ys_ref, gt_ref, lat_ref, m_ref, out_ref, buf_ref, sem):
    def issue(r, carry):
        for k in range(TOP_K):
            _row_copy(ys_ref, dest_ref[0, 0, k * TILE + r], buf_ref.at[k], r, sem).start()
        return carry

    lax.fori_loop(0, TILE, issue, 0)

    def drain(r, carry):
        for k in range(TOP_K):
            _row_copy(ys_ref, 0, buf_ref.at[k], 0, sem).wait()
        return carry

    lax.fori_loop(0, TILE, drain, 0)

    gt = gt_ref[...]
    f = gt[:, 0:1] * buf_ref[0]
    for k in range(1, TOP_K):
        f = f + gt[:, k:k + 1] * buf_ref[k]
    out_ref[...] = lat_ref[...] + m_ref[0][5:6] * f


def _combine_call(dest3, ys, gates_t, lat, mods, nt_lat):
    T, D = lat.shape
    nt = T // TILE
    row = lambda i: (i, 0)
    return pl.pallas_call(
        _combine_kernel,
        grid=(nt,),
        in_specs=[
            pl.BlockSpec((1, 1, TOP_K * TILE), lambda i: (i, 0, 0), memory_space=pltpu.SMEM),
            pl.BlockSpec(memory_space=pl.ANY),
            pl.BlockSpec((TILE, LANE), row),
            pl.BlockSpec((TILE, D), row),
            pl.BlockSpec((1, 6, D), lambda i: (i // nt_lat, 0, 0)),
        ],
        out_specs=pl.BlockSpec((TILE, D), row),
        out_shape=jax.ShapeDtypeStruct((T, D), F32),
        scratch_shapes=[pltpu.VMEM((TOP_K, TILE, D), F32), pltpu.SemaphoreType.DMA(())],
        compiler_params=_params(("arbitrary",)),
        name="combine",
    )(dest3, ys, gates_t, lat, mods)


def _block_diag(blocks):
    H, a, b = blocks.shape
    eye = jnp.eye(H, dtype=blocks.dtype)
    return (eye[:, None, :, None] * blocks[:, :, None, :]).reshape(H * a, H * b)


def _head_slabs(w, width):
    lead = w.shape[:-1]
    w = w.reshape(lead + (MLA_HEADS, width))
    w = jnp.pad(w, [(0, 0)] * len(lead) + [(0, 0), (0, HEAD_PAD - width)])
    return w.reshape(lead + (MLA_HEADS * HEAD_PAD,))


def _rope_tables(rows, n_ctx):
    n_freq = MLA_ROPE // 4
    inv = ROPE_BASE ** (-jnp.arange(n_freq, dtype=F32) / n_freq)
    r = jnp.repeat(jnp.arange(rows, dtype=F32), GRID_W)
    col = jnp.tile(jnp.arange(GRID_W, dtype=F32), rows)
    ang = jnp.concatenate([r[:, None] * inv, col[:, None] * inv], axis=-1)
    cos, sin = jnp.cos(ang), jnp.sin(ang)
    n = cos.shape[0]
    half = MLA_ROPE // 2
    pad = HEAD_PAD - MLA_QK
    c_tab = jnp.concatenate([jnp.ones((n, MLA_NOPE), F32), cos, cos, jnp.ones((n, pad), F32)], axis=1)
    s_tab = jnp.concatenate([jnp.zeros((n, MLA_NOPE), F32), -sin, sin, jnp.zeros((n, pad), F32)], axis=1)
    c_tab = jnp.concatenate([c_tab, jnp.ones((n_ctx, HEAD_PAD), F32)], axis=0)
    s_tab = jnp.concatenate([s_tab, jnp.zeros((n_ctx, HEAD_PAD), F32)], axis=0)
    del half
    return c_tab, s_tab


def kernel(x, c, ctx, c_ctx, w_mod, b_mod, norm_mix, norm_ffn, w_in, w_out, pool_w, pool_scale, mla_q_a_norm, mla_w_uq, mla_kv_a_norm, mla_w_ukv, mla_q_norm, mla_k_norm, sgu_norm_g, sgu_norm_b, sgu_ws, sgu_b, lru_conv_w, lru_conv_b, lru_wa, lru_ba, lru_wx, lru_bx, lru_lambda, router_w, router_b, moe_w_gu, moe_b_gu, moe_w_down, moe_b_down):
    B, N, D = x.shape
    n_ctx = ctx.shape[1]
    L = w_mod.shape[0]
    T = N + n_ctx
    assert B == 1 and N % TILE == 0 and n_ctx == TILE and T % TOK_TILE == 0 and TOK_TILE % SGU_CHUNK == 0
    nt = T // TILE
    nt_lat = N // TILE

    lat = jnp.concatenate([x[0], ctx[0]], axis=0)
    crows = jnp.zeros((8, D), F32).at[0].set(c[0]).at[1].set(c_ctx)
    mods_all = _mod_call(crows, w_mod, b_mod)
    c_tab, s_tab = _rope_tables(N // GRID_W, n_ctx)
    tri = jnp.triu(jnp.ones((TOK_TILE, TOK_TILE), F32), k=1).astype(BF16)

    n_assign = T * TOP_K
    n_blocks = -(-(n_assign + N_EXPERTS * (MOE_BM - 1)) // MOE_BM)
    n_rows = n_blocks * MOE_BM

    for l in range(L):
        mods = mods_all[l, :2].reshape(2, 6, D)

        wa_, wqa, wkva, wkr, wc_, wdx, wdg = jnp.split(
            w_in[l], [256, 448, 576, 608, 1120, 1376], axis=1)
        wkr_placed = jnp.pad(wkr.reshape(D, 1, MLA_ROPE),
                             ((0, 0), (0, 0), (MLA_NOPE, HEAD_PAD - MLA_QK)))
        wkr_placed = jnp.tile(wkr_placed, (1, MLA_HEADS, 1)).reshape(D, MLA_HEADS * HEAD_PAD)
        win = jnp.concatenate(
            [wa_, jnp.pad(wqa, ((0, 0), (0, 256 - MLA_Q_RANK))), wkva, wkr_placed, wc_, wdx, wdg],
            axis=1).astype(BF16)
        assert win.shape[1] == IN_COLS_P
        gqa = jnp.pad(mla_q_a_norm[l], (0, 256 - MLA_Q_RANK)).reshape(1, 256)
        wuq = jnp.pad(_head_slabs(mla_w_uq[l], MLA_QK), ((0, 256 - MLA_Q_RANK), (0, 0))).astype(BF16)
        gkva = mla_kv_a_norm[l].reshape(1, MLA_KV_RANK)
        wukv3 = mla_w_ukv[l].reshape(MLA_KV_RANK, MLA_HEADS, MLA_NOPE + MLA_V)
        wuk = _head_slabs(wukv3[:, :, :MLA_NOPE].reshape(MLA_KV_RANK, -1), MLA_NOPE)
        wuv = _head_slabs(wukv3[:, :, MLA_NOPE:].reshape(MLA_KV_RANK, -1), MLA_V)
        wukv = jnp.concatenate([wuk, wuv], axis=1).astype(BF16)
        gq = jnp.pad(mla_q_norm[l], (0, HEAD_PAD - MLA_QK)).reshape(1, HEAD_PAD)
        gk = jnp.pad(mla_k_norm[l], (0, HEAD_PAD - MLA_QK)).reshape(1, HEAD_PAD)
        wscat = sgu_ws[l].transpose(1, 0, 2).reshape(SGU_CHUNK, SGU_HEADS * SGU_CHUNK).astype(BF16)
        sbias = jnp.repeat(sgu_b[l].T, SGU_HD, axis=1)

        pa, q, k, vt, c_s, xd, gd = _inproj_call(
            lat, mods, norm_mix[l].reshape(1, D), win, gqa, wuq, gkva, wukv, gq, gk, c_tab, s_tab,
            sgu_norm_g[l].reshape(1, GROUP_W), sgu_norm_b[l].reshape(1, GROUP_W), wscat, sbias, N)

        wpool = _block_diag(pool_w[l]).astype(BF16)
        wlru = jnp.concatenate(
            [_block_diag(lru_wa[l, 0]), _block_diag(lru_wx[l, 0]),
             _block_diag(lru_wa[l, 1]), _block_diag(lru_wx[l, 1])], axis=1).astype(BF16)
        blru = jnp.concatenate([lru_ba[l, 0], lru_bx[l, 0], lru_ba[l, 1], lru_bx[l, 1]]).reshape(1, -1)
        a_s, hf, ab, bb = _seq_fwd_call(
            pa, xd, wpool, pool_scale[l].reshape(1, GROUP_W), lru_conv_w[l],
            lru_conv_b[l].reshape(1, GROUP_W), wlru, blru, lru_lambda[l], nt_lat, N, n_ctx)
        d_s = _seq_bwd_call(ab, bb, hf, gd)

        tk = next(t for t in (3328, 1280, TILE) if T % t == 0)
        b_lat = _attn_call(q, k, vt, N, 512, 0, T, tk, 0)
        if l < L - 1:
            b_ctx = _attn_call(q, k, vt, n_ctx, n_ctx, N // n_ctx, n_ctx, n_ctx, N // n_ctx)
        else:
            b_ctx = jnp.zeros((n_ctx, MLA_HEADS * MLA_V), BF16)
        b_s = jnp.concatenate([b_lat, b_ctx], axis=0)

        rw_hi = router_w[l].astype(BF16)
        rw_lo = (router_w[l] - rw_hi.astype(F32)).astype(BF16)
        lane_pad = ((0, 0), (0, LANE - N_EXPERTS))
        rw = jnp.concatenate([jnp.pad(rw_hi, lane_pad), jnp.pad(rw_lo, lane_pad)], axis=1)
        lat2, fin, top_e, gates_t, rank, cnt = _outproj_call(
            a_s, b_s, c_s, d_s, lat, mods, w_out[l].astype(BF16), norm_ffn[l].reshape(1, D),
            rw, router_b[l].reshape(N_EXPERTS, 1), tri, N)

        counts = cnt[:, 0].astype(jnp.int32)
        padded = (counts + MOE_BM - 1) // MOE_BM * MOE_BM
        pad_end = jnp.cumsum(padded)
        pad_start = pad_end - padded
        e_ids = jnp.arange(N_EXPERTS, dtype=jnp.int32)
        start_of = jnp.sum(jnp.where(top_e[None, :TOP_K] == e_ids[:, None, None],
                                     pad_start[:, None, None], 0), axis=0)
        dest = start_of + rank[:TOP_K]
        dest3 = dest.reshape(TOP_K, nt, TILE).transpose(1, 0, 2).reshape(nt, 1, TOP_K * TILE)
        block_row0 = jnp.arange(n_blocks, dtype=jnp.int32) * MOE_BM
        block_e = jnp.minimum(
            jnp.sum((pad_end[None, :] <= block_row0[:, None]).astype(jnp.int32), axis=1),
            N_EXPERTS - 1)
        n_used = (pad_end[-1:] // MOE_BM).astype(jnp.int32)

        xs = _dispatch_call(pad_start + counts, pad_end, dest3, fin, n_rows)
        ys = _expert_call(block_e, n_used, xs, moe_w_gu, moe_b_gu, moe_w_down, moe_b_down, l)
        lat = _combine_call(dest3, ys, gates_t, lat2, mods, nt_lat)

    return lat[:N].reshape(B, N, D)
```

```python
import functools

import jax
import jax.numpy as jnp
from jax import lax
from jax.experimental import pallas as pl
from jax.experimental.pallas import tpu as pltpu

F32 = jnp.float32
BF16 = jnp.bfloat16
HIGHEST = lax.Precision.HIGHEST

EPS = 1e-6
LOG2_E = 1.4426950408889634
GRID_W = 64
GROUP_W = 256
POOL_WINDOWS = (2, 4, 8, 16)
POOL_CH = GROUP_W // len(POOL_WINDOWS)
MLA_HEADS = 4
MLA_NOPE = 64
MLA_ROPE = 32
MLA_QK = MLA_NOPE + MLA_ROPE
MLA_V = 64
MLA_Q_RANK = 192
MLA_KV_RANK = 128
ROPE_BASE = 10000.0
SGU_HEADS = 4
SGU_HD = GROUP_W // SGU_HEADS
SGU_CHUNK = 128
LRU_HEADS = 4
LRU_HD = GROUP_W // LRU_HEADS
CONV_W = 4
CONV_LEFT = 1
LRU_C = 8.0
N_EXPERTS = 32
TOP_K = 4
SWIGLU_LIMIT = 7.0
SWIGLU_ALPHA = 1.702

LANE = 128
HEAD_PAD = LANE
TILE = 256
TOK_TILE = 640
HALO = 16
ATTN_CHUNK = 256
MOE_BM = 256
VMEM_LIMIT = 56 * 1024 * 1024

C_A = 0
C_QA = C_A + GROUP_W
C_KVA = C_QA + 256
C_KR = C_KVA + MLA_KV_RANK
C_CU = C_KR + MLA_HEADS * HEAD_PAD
C_CV = C_CU + GROUP_W
C_DX = C_CV + GROUP_W
C_DG = C_DX + GROUP_W
IN_COLS_P = C_DG + GROUP_W


def _params(sem, vmem=VMEM_LIMIT):
    return pltpu.CompilerParams(dimension_semantics=sem, vmem_limit_bytes=vmem)


def _rms(x, n):
    return x * lax.rsqrt(jnp.sum(x * x, axis=-1, keepdims=True) * (1.0 / n) + EPS)


def _mod_kernel(c_ref, w_ref, b_ref, o_ref):
    cv = c_ref[...]
    s = cv * jax.nn.sigmoid(cv)
    o_ref[0] = jnp.dot(s, w_ref[0], precision=HIGHEST, preferred_element_type=F32) + b_ref[0]


def _mod_call(crows, w_mod, b_mod):
    L, D, M = w_mod.shape
    bn = 1536
    return pl.pallas_call(
        _mod_kernel,
        grid=(L, M // bn),
        in_specs=[
            pl.BlockSpec((8, D), lambda l, j: (0, 0)),
            pl.BlockSpec((1, D, bn), lambda l, j: (l, 0, j)),
            pl.BlockSpec((1, 1, bn), lambda l, j: (l, 0, j)),
        ],
        out_specs=pl.BlockSpec((1, 8, bn), lambda l, j: (l, 0, j)),
        out_shape=jax.ShapeDtypeStruct((L, 8, M), F32),
        compiler_params=_params(("arbitrary", "arbitrary")),
        name="mod",
    )(crows, w_mod, b_mod.reshape(L, 1, M))


def _rope_heads(x, gain, c_tab, s_tab, lane, scale):
    outs = []
    for h in range(MLA_HEADS):
        xh = x[:, h * HEAD_PAD:(h + 1) * HEAD_PAD]
        xn = _rms(xh, MLA_QK) * gain
        swap = jnp.where(lane < MLA_NOPE + MLA_ROPE // 2,
                         pltpu.roll(xn, HEAD_PAD - MLA_ROPE // 2, axis=1),
                         pltpu.roll(xn, MLA_ROPE // 2, axis=1))
        r = xn * c_tab + swap * s_tab
        if scale != 1.0:
            r = r * scale
        outs.append(r.astype(BF16))
    return jnp.concatenate(outs, axis=1)


def _mod_row(m_ref, is_ctx, k):
    return jnp.where(is_ctx, m_ref[1, k:k + 1, :], m_ref[0, k:k + 1, :])


def _is_ctx_rows(tm, n_lat):
    row = pl.program_id(0) * tm + lax.broadcasted_iota(jnp.int32, (tm, 1), 0)
    return row >= n_lat


def _inproj_kernel(n_lat, x_ref, m_ref, gn_ref, win_ref, gqa_ref, wuq_ref, gkva_ref, wukv_ref,
                   gq_ref, gk_ref, ct_ref, st_ref, sg_ref, sb_ref, ws_ref, sbias_ref,
                   pa_ref, q_ref, k_ref, vt_ref, c_ref, xd_ref, gd_ref):
    x = x_ref[...]
    is_ctx = _is_ctx_rows(x.shape[0], n_lat)
    h = _rms(x, x.shape[-1]) * gn_ref[...]
    h = h * (1.0 + _mod_row(m_ref, is_ctx, 1)) + _mod_row(m_ref, is_ctx, 0)
    p = jnp.dot(h.astype(BF16), win_ref[...], preferred_element_type=F32)

    pa_ref[...] = p[:, C_A:C_A + GROUP_W]
    xd_ref[...] = p[:, C_DX:C_DX + GROUP_W]
    gd_ref[...] = jax.nn.gelu(p[:, C_DG:C_DG + GROUP_W])

    lane = lax.broadcasted_iota(jnp.int32, (x.shape[0], HEAD_PAD), 1)
    c_tab = ct_ref[...]
    s_tab = st_ref[...]
    qa = _rms(p[:, C_QA:C_QA + 256], MLA_Q_RANK) * gqa_ref[...]
    q = jnp.dot(qa.astype(BF16), wuq_ref[...], preferred_element_type=F32)
    q_ref[...] = _rope_heads(q, gq_ref[...], c_tab, s_tab, lane, MLA_QK ** -0.5 * LOG2_E)

    kva = _rms(p[:, C_KVA:C_KVA + MLA_KV_RANK], MLA_KV_RANK) * gkva_ref[...]
    kv = jnp.dot(kva.astype(BF16), wukv_ref[...], preferred_element_type=F32)
    kpre = kv[:, :MLA_HEADS * HEAD_PAD] + p[:, C_KR:C_KR + MLA_HEADS * HEAD_PAD]
    k_ref[...] = _rope_heads(kpre, gk_ref[...], c_tab, s_tab, lane, 1.0)
    v = kv[:, MLA_HEADS * HEAD_PAD:]
    slab_lane = lax.broadcasted_iota(jnp.int32, v.shape, 1) % HEAD_PAD
    vt_ref[...] = jnp.where(slab_lane == MLA_V, 1.0, v).T.astype(BF16)

    z = jax.nn.gelu(p[:, C_CU:C_CU + 2 * GROUP_W])
    u = z[:, :GROUP_W]
    v = z[:, GROUP_W:]
    mu = jnp.mean(v, axis=-1, keepdims=True)
    vc = v - mu
    var = jnp.mean(vc * vc, axis=-1, keepdims=True)
    vn = (vc * lax.rsqrt(var + EPS) * sg_ref[...] + sb_ref[...]).astype(BF16)
    head_of_lane = lax.broadcasted_iota(jnp.int32, (SGU_CHUNK, GROUP_W), 1) // SGU_HD
    zero = jnp.zeros((SGU_CHUNK, GROUP_W), BF16)
    for cch in range(x.shape[0] // SGU_CHUNK):
        rows = slice(cch * SGU_CHUNK, (cch + 1) * SGU_CHUNK)
        vch = vn[rows]
        stacked = jnp.concatenate(
            [jnp.where(head_of_lane == hh, vch, zero) for hh in range(SGU_HEADS)], axis=0)
        mixed = jnp.dot(ws_ref[...], stacked, preferred_element_type=F32) + sbias_ref[...]
        c_ref[rows, :] = (u[rows] * mixed).astype(BF16)


def _inproj_call(xcat, mods, gn, win, gqa, wuq, gkva, wukv, gq, gk, ctab, stab, sg, sb, wscat, sbias,
                 n_lat):
    T, D = xcat.shape
    tm = TILE
    nt = T // tm
    row = lambda i: (i, 0)

    def full(a):
        return pl.BlockSpec(a.shape, lambda i: (0,) * a.ndim)

    out_shapes = (
        jax.ShapeDtypeStruct((T, GROUP_W), F32),
        jax.ShapeDtypeStruct((T, MLA_HEADS * HEAD_PAD), BF16),
        jax.ShapeDtypeStruct((T, MLA_HEADS * HEAD_PAD), BF16),
        jax.ShapeDtypeStruct((MLA_HEADS * HEAD_PAD, T), BF16),
        jax.ShapeDtypeStruct((T, GROUP_W), BF16),
        jax.ShapeDtypeStruct((T, GROUP_W), F32),
        jax.ShapeDtypeStruct((T, GROUP_W), F32),
    )
    out_specs = [pl.BlockSpec((tm, s.shape[1]), row) for s in out_shapes]
    out_specs[3] = pl.BlockSpec((MLA_HEADS * HEAD_PAD, tm), lambda i: (0, i))
    return pl.pallas_call(
        functools.partial(_inproj_kernel, n_lat),
        grid=(nt,),
        in_specs=[
            pl.BlockSpec((tm, D), row),
            full(mods),
            full(gn), full(win), full(gqa), full(wuq), full(gkva), full(wukv), full(gq), full(gk),
            pl.BlockSpec((tm, HEAD_PAD), row),
            pl.BlockSpec((tm, HEAD_PAD), row),
            full(sg), full(sb), full(wscat), full(sbias),
        ],
        out_specs=tuple(out_specs),
        out_shape=out_shapes,
        compiler_params=_params(("parallel",)),
        name="inproj",
    )(xcat, mods, gn, win, gqa, wuq, gkva, wukv, gq, gk, ctab, stab, sg, sb, wscat, sbias)


def _shift_rows(x, d, fill, reverse):
    n = x.shape[0]
    t = lax.broadcasted_iota(jnp.int32, x.shape, 0)
    if reverse:
        return jnp.where(t < n - d, pltpu.roll(x, n - d, axis=0), fill)
    return jnp.where(t >= d, pltpu.roll(x, d, axis=0), fill)


def _tile_scan(a, b, reverse):
    d = 1
    while d < a.shape[0]:
        a_s = _shift_rows(a, d, 1.0, reverse)
        b_s = _shift_rows(b, d, 0.0, reverse)
        b = a * b_s + b
        a = a * a_s
        d *= 2
    return a, b


def _lru_coeffs(xc, proj, sp, d):
    r = jax.nn.sigmoid(proj[:, (2 * d) * GROUP_W:(2 * d + 1) * GROUP_W])
    i = jax.nn.sigmoid(proj[:, (2 * d + 1) * GROUP_W:(2 * d + 2) * GROUP_W])
    log_a = -LRU_C * r * sp[d:d + 1]
    a = jnp.exp(log_a)
    drive = jnp.sqrt(1.0 - jnp.exp(2.0 * log_a)) * (i * xc)
    return a, drive


def _seq_fwd_kernel(nt_lat, n_lat, n_ctx,
                    pa_ref, pa_prev_ref, pa_next_ref, xd_ref, xd_prev_ref, xd_next_ref,
                    wpool_ref, pscale_ref, cw_ref, cb_ref, wlru_ref, blru_ref, lam_ref,
                    a_out_ref, hf_ref, ab_ref, bb_ref,
                    ext_ref, carry_ref):
    j = pl.program_id(0)
    nt = pl.num_programs(0)
    ti = (j + nt_lat) % nt
    is_ctx = ti >= nt_lat
    seq_first = jnp.logical_or(ti == 0, ti == nt_lat)
    seq_last = jnp.logical_or(ti == nt_lat - 1, ti == nt - 1)
    t_loc = (ti - jnp.where(is_ctx, nt_lat, 0)) * TILE
    n_seq = jnp.where(is_ctx, n_ctx, n_lat)

    @pl.when(j == 0)
    def _():
        carry_ref[...] = jnp.zeros_like(carry_ref)

    def load_ext(cur_ref, prev_ref, next_ref):
        ext_ref[0:HALO, :] = jnp.where(seq_first, 0.0, prev_ref[...])
        ext_ref[HALO:HALO + TILE, :] = cur_ref[...]
        ext_ref[HALO + TILE:, :] = jnp.where(seq_last, 0.0, next_ref[...])

    def win(off):
        return ext_ref[pl.ds(HALO + off, TILE), :]

    load_ext(pa_ref, pa_prev_ref, pa_next_ref)
    x = pa_ref[...]
    t = t_loc + lax.broadcasted_iota(jnp.int32, (TILE, GROUP_W), 0)
    lane = lax.broadcasted_iota(jnp.int32, (TILE, GROUP_W), 1)
    mean = jnp.zeros((TILE, GROUP_W), F32)
    for g, w in enumerate(POOL_WINDOWS):
        acc = win(-(w // 2))
        for off in range(-(w // 2) + 1, w // 2):
            acc = acc + win(off)
        cnt = jnp.minimum(t + w // 2, n_seq) - jnp.maximum(t - w // 2, 0)
        mean = jnp.where(lane // POOL_CH == g, acc / cnt.astype(F32), mean)
    diff = (mean - x).astype(BF16)
    pooled = jnp.dot(diff, wpool_ref[...], preferred_element_type=F32) * pscale_ref[...]
    a_out_ref[...] = pooled.astype(BF16)

    load_ext(xd_ref, xd_prev_ref, xd_next_ref)
    xc = jnp.zeros((TILE, GROUP_W), F32) + cb_ref[...]
    for k in range(CONV_W):
        xc = xc + win(k - CONV_LEFT) * cw_ref[k:k + 1, :]

    proj = jnp.dot(xc.astype(BF16), wlru_ref[...], preferred_element_type=F32) + blru_ref[...]
    lam = lam_ref[...]
    sp = jnp.maximum(-lam, 0.0) + jnp.log(1.0 + jnp.exp(-jnp.abs(lam)))
    a_f, b_f = _lru_coeffs(xc, proj, sp, 0)
    a_b, b_b = _lru_coeffs(xc, proj, sp, 1)
    ab_ref[...] = a_b
    bb_ref[...] = b_b

    big_a, big_b = _tile_scan(a_f, b_f, reverse=False)
    hf = big_b + big_a * carry_ref[0:1, :]
    hf_ref[...] = hf
    carry_ref[0:1, :] = hf[TILE - 1:TILE, :]


def _seq_fwd_call(pa, xd, wpool, pscale, cw, cb, wlru, blru, lam, nt_lat, n_lat, n_ctx):
    T = pa.shape[0]
    nt = T // TILE
    hb = TILE // HALO
    n_halo = T // HALO

    def tile_of(j):
        return (j + nt_lat) % nt

    cur = lambda j: (tile_of(j), 0)
    prev = lambda j: (jnp.maximum(tile_of(j) * hb - 1, 0), 0)
    nxt = lambda j: (jnp.minimum((tile_of(j) + 1) * hb, n_halo - 1), 0)
    fixed = lambda j: (0, 0)

    def full(a):
        return pl.BlockSpec(a.shape, fixed)

    tile_spec = pl.BlockSpec((TILE, GROUP_W), cur)
    out_shapes = (
        jax.ShapeDtypeStruct((T, GROUP_W), BF16),
        jax.ShapeDtypeStruct((T, GROUP_W), F32),
        jax.ShapeDtypeStruct((T, GROUP_W), F32),
        jax.ShapeDtypeStruct((T, GROUP_W), F32),
    )
    return pl.pallas_call(
        functools.partial(_seq_fwd_kernel, nt_lat, n_lat, n_ctx),
        grid=(nt,),
        in_specs=[
            tile_spec, pl.BlockSpec((HALO, GROUP_W), prev), pl.BlockSpec((HALO, GROUP_W), nxt),
            tile_spec, pl.BlockSpec((HALO, GROUP_W), prev), pl.BlockSpec((HALO, GROUP_W), nxt),
            full(wpool), full(pscale), full(cw), full(cb), full(wlru), full(blru), full(lam),
        ],
        out_specs=tuple(pl.BlockSpec((TILE, GROUP_W), cur) for _ in out_shapes),
        out_shape=out_shapes,
        scratch_shapes=[pltpu.VMEM((TILE + 2 * HALO, GROUP_W), F32), pltpu.VMEM((8, GROUP_W), F32)],
        compiler_params=_params(("arbitrary",)),
        name="seq_fwd",
    )(pa, pa, pa, xd, xd, xd, wpool, pscale, cw, cb, wlru, blru, lam)


def _seq_bwd_kernel(ab_ref, bb_ref, hf_ref, gd_ref, d_out_ref, carry_ref):
    @pl.when(pl.program_id(0) == 0)
    def _():
        carry_ref[...] = jnp.zeros_like(carry_ref)

    big_a, big_b = _tile_scan(ab_ref[...], bb_ref[...], reverse=True)
    hb = big_b + big_a * carry_ref[0:1, :]
    carry_ref[0:1, :] = hb[0:1, :]
    d_out_ref[...] = (gd_ref[...] * (hf_ref[...] + hb)).astype(BF16)


def _seq_bwd_call(ab, bb, hf, gd):
    T = ab.shape[0]
    nt = T // TILE
    spec = pl.BlockSpec((TILE, GROUP_W), lambda j: (nt - 1 - j, 0))
    return pl.pallas_call(
        _seq_bwd_kernel,
        grid=(nt,),
        in_specs=[spec, spec, spec, spec],
        out_specs=spec,
        out_shape=jax.ShapeDtypeStruct((T, GROUP_W), BF16),
        scratch_shapes=[pltpu.VMEM((8, GROUP_W), F32)],
        compiler_params=_params(("arbitrary",)),
        name="seq_bwd",
    )(ab, bb, hf, gd)


def _attn_kernel(q_ref, k_ref, vt_ref, o_ref, m_ref, acc_ref):
    kv = pl.program_id(1)

    @pl.when(kv == 0)
    def _():
        m_ref[...] = jnp.full_like(m_ref, -jnp.inf)
        acc_ref[...] = jnp.zeros_like(acc_ref)

    def head_cols(h):
        return slice(h * HEAD_PAD, (h + 1) * HEAD_PAD)

    ck = ATTN_CHUNK if k_ref.shape[0] % ATTN_CHUNK == 0 else k_ref.shape[0]
    items = [(c, h) for c in range(k_ref.shape[0] // ck) for h in range(MLA_HEADS)]

    def scores(c, h):
        cols = head_cols(h)
        return lax.dot_general(k_ref[c * ck:(c + 1) * ck, cols], q_ref[:, cols], (((1,), (1,)), ((), ())),
                               preferred_element_type=F32)

    def probs(h, s):
        m_prev = m_ref[h]
        m_new = jnp.maximum(m_prev, jnp.max(s, axis=0, keepdims=True))
        m_ref[h] = m_new
        return jnp.exp2(m_prev - m_new), jnp.exp2(s - m_new).astype(BF16)

    def accumulate(c, h, alpha, p):
        acc_ref[h] = alpha * acc_ref[h] + jnp.dot(vt_ref[head_cols(h), c * ck:(c + 1) * ck], p,
                                                  preferred_element_type=F32)

    s_next = scores(*items[0])
    pending = None
    for n, (c, h) in enumerate(items):
        s_cur = s_next
        if n + 1 < len(items):
            s_next = scores(*items[n + 1])
        if pending is not None:
            accumulate(*pending)
        pending = (c, h) + probs(h, s_cur)
    accumulate(*pending)

    @pl.when(kv == pl.num_programs(1) - 1)
    def _():
        outs = []
        for h in range(MLA_HEADS):
            a = acc_ref[h]
            outs.append((a / a[MLA_V:MLA_V + 1, :]).T[:, :MLA_V])
        o_ref[...] = jnp.concatenate(outs, axis=1).astype(BF16)


def _attn_call(q, k, vt, n_q, tq, q_blk0, n_k, tk, k_blk0):
    width = MLA_HEADS * HEAD_PAD
    return pl.pallas_call(
        _attn_kernel,
        grid=(n_q // tq, n_k // tk),
        in_specs=[
            pl.BlockSpec((tq, width), lambda i, j: (q_blk0 + i, 0)),
            pl.BlockSpec((tk, width), lambda i, j: (k_blk0 + j, 0)),
            pl.BlockSpec((width, tk), lambda i, j: (0, k_blk0 + j)),
        ],
        out_specs=pl.BlockSpec((tq, MLA_HEADS * MLA_V), lambda i, j: (i, 0)),
        out_shape=jax.ShapeDtypeStruct((n_q, MLA_HEADS * MLA_V), BF16),
        scratch_shapes=[
            pltpu.VMEM((MLA_HEADS, 1, tq), F32),
            pltpu.VMEM((MLA_HEADS, HEAD_PAD, tq), F32),
        ],
        compiler_params=_params(("parallel", "arbitrary")),
        name="attn",
    )(q, k, vt)


def _outproj_kernel(n_lat, a_ref, b_ref, c_ref, d_ref, x_ref, m_ref, wout_ref, gn_ref, rw_ref, rb_ref,
                    lat_ref, fin_ref, tope_ref, gatet_ref):
    tm = x_ref.shape[0]
    n_e = rb_ref.shape[0]
    is_ctx = _is_ctx_rows(tm, n_lat)
    y = jnp.dot(jnp.concatenate([a_ref[...], b_ref[...], c_ref[...], d_ref[...]], axis=1),
                wout_ref[...], preferred_element_type=F32)
    lat = x_ref[...] + _mod_row(m_ref, is_ctx, 2) * y
    lat_ref[...] = lat
    f = _rms(lat, lat.shape[-1]) * gn_ref[...]
    f = f * (1.0 + _mod_row(m_ref, is_ctx, 4)) + _mod_row(m_ref, is_ctx, 3)

    f_hi = f.astype(BF16)
    fin_ref[...] = f_hi
    f_lo = (f - f_hi.astype(F32)).astype(BF16)
    hi_prod = jnp.dot(f_hi, rw_ref[...], preferred_element_type=F32)
    lo_prod = jnp.dot(f_lo, rw_ref[:, :LANE], preferred_element_type=F32)
    logits_rows = hi_prod[:, :LANE] + hi_prod[:, LANE:] + lo_prod
    logits = logits_rows.T[:n_e] + rb_ref[...]
    e_iota = lax.broadcasted_iota(jnp.int32, (n_e, tm), 0)
    vals, idxs = [], []
    for _ in range(TOP_K):
        mx = jnp.max(logits, axis=0, keepdims=True)
        idx = jnp.min(jnp.where(logits == mx, e_iota, n_e), axis=0, keepdims=True)
        vals.append(mx)
        idxs.append(idx)
        logits = jnp.where(e_iota == idx, -jnp.inf, logits)
    ex = [jnp.exp(vv - vals[0]) for vv in vals]
    den = ex[0] + ex[1] + ex[2] + ex[3]
    row8 = lax.broadcasted_iota(jnp.int32, (8, tm), 0)
    row128 = lax.broadcasted_iota(jnp.int32, (LANE, tm), 0)
    tope = jnp.zeros((8, tm), jnp.int32)
    gates = jnp.zeros((LANE, tm), F32)
    for k in range(TOP_K):
        tope = jnp.where(row8 == k, idxs[k], tope)
        gates = jnp.where(row128 == k, ex[k] / den, gates)
    tope_ref[...] = tope
    gatet_ref[...] = gates.T


def _outproj_call(a, b, c, d, xcat, mods, wout, gn, rw, rb, n_lat):
    T, D = xcat.shape
    tm = TOK_TILE
    row = lambda i: (i, 0)

    def full(arr):
        return pl.BlockSpec(arr.shape, lambda i: (0,) * arr.ndim)

    g_spec = pl.BlockSpec((tm, GROUP_W), row)
    out_shapes = (
        jax.ShapeDtypeStruct((T, D), F32),
        jax.ShapeDtypeStruct((T, D), BF16),
        jax.ShapeDtypeStruct((8, T), jnp.int32),
        jax.ShapeDtypeStruct((T, LANE), F32),
    )
    out_specs = (
        pl.BlockSpec((tm, D), row), pl.BlockSpec((tm, D), row),
        pl.BlockSpec((8, tm), lambda i: (0, i)),
        pl.BlockSpec((tm, LANE), row),
    )
    return pl.pallas_call(
        functools.partial(_outproj_kernel, n_lat),
        grid=(T // tm,),
        in_specs=[
            g_spec, g_spec, g_spec, g_spec,
            pl.BlockSpec((tm, D), row),
            full(mods), full(wout), full(gn), full(rw), full(rb),
        ],
        out_specs=out_specs,
        out_shape=out_shapes,
        compiler_params=_params(("parallel",)),
        name="outproj",
    )(a, b, c, d, xcat, mods, wout, gn, rw, rb)


SEG = 8
SORT_ROWS = TOP_K * TILE + N_EXPERTS * SEG
SEG_START, SEG_COUNT, SEG_DEST = 0, N_EXPERTS, 2 * N_EXPERTS


def _plan_kernel(tope_ref, tri_ref, ltri_ref, lp_ref, lpt_ref, seg_ref, tot_ref, carry_ref):
    @pl.when(pl.program_id(0) == 0)
    def _():
        carry_ref[...] = jnp.zeros_like(carry_ref)

    tope = tope_ref[...]
    n_e = ltri_ref.shape[0]
    tm = tope.shape[1]
    e_iota = lax.broadcasted_iota(jnp.int32, (n_e, tm), 0)
    onehot = jnp.concatenate(
        [jnp.where(e_iota == tope[k:k + 1, :], 1.0, 0.0) for k in range(TOP_K)], axis=0)
    prefix = jnp.dot(onehot.astype(BF16), tri_ref[...], preferred_element_type=F32)
    cnt_k = [jnp.sum(onehot[k * n_e:(k + 1) * n_e], axis=1, keepdims=True) for k in range(TOP_K)]
    cnt = cnt_k[0] + cnt_k[1] + cnt_k[2] + cnt_k[3]
    units = jnp.floor((cnt + (SEG - 1)) * (1.0 / SEG))
    units_b = jnp.broadcast_to(units, (n_e, LANE))
    start_u = jnp.dot(ltri_ref[...], units_b.astype(BF16), preferred_element_type=F32)
    base = start_u[:, 0:1] * SEG
    row8 = lax.broadcasted_iota(jnp.int32, (8, tm), 0)
    row128 = lax.broadcasted_iota(jnp.int32, (LANE, tm), 0)
    lp8 = jnp.zeros((8, tm), F32)
    lp128 = jnp.zeros((LANE, tm), F32)
    for k in range(TOP_K):
        hot = onehot[k * n_e:(k + 1) * n_e]
        lp_k = jnp.sum(hot * (prefix[k * n_e:(k + 1) * n_e] + base), axis=0, keepdims=True)
        lp8 = jnp.where(row8 == k, lp_k, lp8)
        lp128 = jnp.where(row128 == k, lp_k, lp128)
        base = base + cnt_k[k]
    lp_ref[...] = lp8.astype(jnp.int32)
    lpt_ref[...] = lp128.T.astype(jnp.int32)
    lane = lax.broadcasted_iota(jnp.int32, (n_e, LANE), 1)
    carry = carry_ref[...]
    seg = jnp.where(lane == 0, start_u, jnp.where(lane == 1, units_b, jnp.where(lane == 2, carry, 0.0)))
    seg_ref[...] = seg.astype(jnp.int32)
    carry_ref[...] = carry + units_b
    tot_ref[...] = carry + units_b


def _plan_call(top_e, tri, ltri):
    T = top_e.shape[1]
    nt = T // TILE
    n_e = ltri.shape[0]
    fixed = lambda i: (0, 0)
    out_shapes = (
        jax.ShapeDtypeStruct((8, T), jnp.int32),
        jax.ShapeDtypeStruct((T, LANE), jnp.int32),
        jax.ShapeDtypeStruct((nt * n_e, LANE), jnp.int32),
        jax.ShapeDtypeStruct((n_e, LANE), F32),
    )
    return pl.pallas_call(
        _plan_kernel,
        grid=(nt,),
        in_specs=[pl.BlockSpec((8, TILE), lambda i: (0, i)), pl.BlockSpec(tri.shape, fixed),
                  pl.BlockSpec(ltri.shape, fixed)],
        out_specs=(pl.BlockSpec((8, TILE), lambda i: (0, i)), pl.BlockSpec((TILE, LANE), lambda i: (i, 0)),
                   pl.BlockSpec((n_e, LANE), lambda i: (i, 0)), pl.BlockSpec((n_e, LANE), fixed)),
        out_shape=out_shapes,
        scratch_shapes=[pltpu.VMEM((n_e, LANE), F32)],
        compiler_params=_params(("arbitrary",)),
        name="plan",
    )(top_e, tri, ltri)


def _chunk_copy(src, s_row, dst, d_row, sem):
    return pltpu.make_async_copy(src.at[pl.ds(pl.multiple_of(s_row, SEG), SEG)],
                                 dst.at[pl.ds(pl.multiple_of(d_row, SEG), SEG)], sem)


def _for_each_chunk(seg_ref, fn):
    def per_expert(e, total):
        n = seg_ref[0, 0, SEG_COUNT + e]
        local = seg_ref[0, 0, SEG_START + e] * SEG
        dest = seg_ref[0, 0, SEG_DEST + e] * SEG

        def per_chunk(j, carry):
            fn(local + j * SEG, dest + j * SEG)
            return carry

        lax.fori_loop(0, n, per_chunk, 0)
        return total + n

    return lax.fori_loop(0, N_EXPERTS, per_expert, 0)


def _dispatch_kernel(lo_ref, hi_ref, seg_ref, lp_ref, fin_ref, xs_ref, sorted_ref, zero_ref, cnt_ref, sems, zero_sem,
                     free_sem):
    i = pl.program_id(0)
    slot = i % 2

    @pl.when(i == 0)
    def _():
        cnt_ref[0] = 0
        cnt_ref[1] = 0
        zero_ref[...] = jnp.zeros_like(zero_ref)

        def per_expert(e, carry):
            def start(u, c):
                _chunk_copy(zero_ref, 0, xs_ref, u * SEG, zero_sem).start()
                return c

            def wait(u, c):
                _chunk_copy(zero_ref, 0, xs_ref, 0, zero_sem).wait()
                return c

            lax.fori_loop(lo_ref[e], hi_ref[e], start, 0)
            lax.fori_loop(lo_ref[e], hi_ref[e], wait, 0)
            return carry

        lax.fori_loop(0, N_EXPERTS, per_expert, 0)

    first_free = hi_ref[N_EXPERTS - 1] // (MOE_BM // SEG)
    n_blocks = xs_ref.shape[0] // MOE_BM

    def free_block_copy(b):
        return pltpu.make_async_copy(zero_ref, xs_ref.at[pl.ds(pl.multiple_of(b * MOE_BM, MOE_BM), MOE_BM)], free_sem)

    @pl.when(i == 0)
    def _():
        def start(b, c):
            free_block_copy(b).start()
            return c

        lax.fori_loop(first_free, n_blocks, start, 0)

    def drain(s):
        def wait(j, c):
            _chunk_copy(sorted_ref.at[s], 0, xs_ref, 0, sems.at[s]).wait()
            return c

        lax.fori_loop(0, cnt_ref[s], wait, 0)

    drain(slot)
    lp = lp_ref[...]
    j_iota = lax.broadcasted_iota(jnp.int32, (SORT_ROWS, lp.shape[1]), 0)
    perm = jnp.zeros(j_iota.shape, F32)
    for k in range(TOP_K):
        perm = jnp.where(j_iota == lp[k:k + 1, :], 1.0, perm)
    sorted_ref[slot] = jnp.dot(perm.astype(BF16), fin_ref[...], preferred_element_type=F32)

    def send(local, dest):
        _chunk_copy(sorted_ref.at[slot], local, xs_ref, dest, sems.at[slot]).start()

    cnt_ref[slot] = _for_each_chunk(seg_ref, send)

    @pl.when(i == pl.num_programs(0) - 1)
    def _():
        drain(slot)
        drain(1 - slot)

        def wait(b, c):
            free_block_copy(b).wait()
            return c

        lax.fori_loop(first_free, n_blocks, wait, 0)


def _dispatch_call(pad_lo, pad_hi, seg3, lp, fin, n_rows):
    nt = seg3.shape[0]
    D = fin.shape[1]
    grid_spec = pltpu.PrefetchScalarGridSpec(
        num_scalar_prefetch=2,
        grid=(nt,),
        in_specs=[
            pl.BlockSpec((1, 1, LANE), lambda i, lo, hi: (i, 0, 0), memory_space=pltpu.SMEM),
            pl.BlockSpec((8, TILE), lambda i, lo, hi: (0, i)),
            pl.BlockSpec((TILE, D), lambda i, lo, hi: (i, 0)),
        ],
        out_specs=pl.BlockSpec(memory_space=pl.ANY),
        scratch_shapes=[pltpu.VMEM((2, SORT_ROWS, D), F32), pltpu.VMEM((MOE_BM, D), F32), pltpu.SMEM((2,), jnp.int32),
                        pltpu.SemaphoreType.DMA((2,)), pltpu.SemaphoreType.DMA(()), pltpu.SemaphoreType.DMA(())],
    )
    return pl.pallas_call(
        _dispatch_kernel,
        grid_spec=grid_spec,
        out_shape=jax.ShapeDtypeStruct((n_rows, D), F32),
        compiler_params=_params(("arbitrary",)),
        name="dispatch",
    )(pad_lo, pad_hi, seg3, lp, fin)


def _expert_kernel(be_ref, nu_ref, xs_ref, wgu_ref, bgu_ref, wdn_ref, bdn_ref, ys_ref, wgu_bf, wdn_bf):
    i = pl.program_id(0)
    changed = jnp.logical_or(i == 0, be_ref[i] != be_ref[jnp.maximum(i - 1, 0)])

    @pl.when(changed)
    def _():
        wgu_bf[...] = wgu_ref[...].astype(BF16)
        wdn_bf[...] = wdn_ref[...].astype(BF16)

    @pl.when(i < nu_ref[0])
    def _():
        f = wdn_bf.shape[0]
        gu = jnp.dot(xs_ref[...].astype(BF16), wgu_bf[...], preferred_element_type=F32) + bgu_ref[...]
        g = jnp.minimum(gu[:, :f], SWIGLU_LIMIT)
        u = jnp.clip(gu[:, f:], -SWIGLU_LIMIT, SWIGLU_LIMIT)
        act = (u + 1.0) * (g * jax.nn.sigmoid(SWIGLU_ALPHA * g))
        ys_ref[...] = jnp.dot(act.astype(BF16), wdn_bf[...], preferred_element_type=F32) + bdn_ref[...]

    @pl.when(i >= nu_ref[0])
    def _():
        ys_ref[...] = jnp.zeros_like(ys_ref)


def _expert_call(block_e, n_used, xs, w_gu, b_gu, w_down, b_down, layer):
    n_rows, D = xs.shape
    _, E, _, F2 = w_gu.shape
    F = F2 // 2
    grid_spec = pltpu.PrefetchScalarGridSpec(
        num_scalar_prefetch=2,
        grid=(n_rows // MOE_BM,),
        in_specs=[
            pl.BlockSpec((MOE_BM, D), lambda i, be, nu: (jnp.minimum(i, nu[0] - 1), 0)),
            pl.BlockSpec((None, None, D, F2), lambda i, be, nu: (layer, be[i], 0, 0)),
            pl.BlockSpec((None, None, 1, F2), lambda i, be, nu: (layer, be[i], 0, 0)),
            pl.BlockSpec((None, None, F, D), lambda i, be, nu: (layer, be[i], 0, 0)),
            pl.BlockSpec((None, None, 1, D), lambda i, be, nu: (layer, be[i], 0, 0)),
        ],
        out_specs=pl.BlockSpec((MOE_BM, D), lambda i, be, nu: (i, 0)),
        scratch_shapes=[pltpu.VMEM((D, F2), BF16), pltpu.VMEM((F, D), BF16)],
    )
    L = w_gu.shape[0]
    return pl.pallas_call(
        _expert_kernel,
        grid_spec=grid_spec,
        out_shape=jax.ShapeDtypeStruct((n_rows, D), F32),
        compiler_params=_params(("arbitrary",)),
        name="expert",
    )(block_e, n_used, xs, w_gu, b_gu.reshape(L, E, 1, F2), w_down, b_down.reshape(L, E, 1, D))


def _combine_kernel(seg_ref, seg_next_ref, lpt_ref, gt_ref, ys_ref, lat_ref, m_ref, out_ref, ysort_ref, sems):
    i = pl.program_id(0)
    slot = i % 2

    def fetch(seg, s):
        def recv(local, src):
            _chunk_copy(ys_ref, src, ysort_ref.at[s], local, sems.at[s]).start()

        _for_each_chunk(seg, recv)

    @pl.when(i == 0)
    def _():
        ysort_ref[...] = jnp.zeros_like(ysort_ref)
        fetch(seg_ref, 0)

    @pl.when(i + 1 < pl.num_programs(0))
    def _():
        fetch(seg_next_ref, 1 - slot)

    def count(e, total):
        return total + seg_ref[0, 0, SEG_COUNT + e]

    def wait(j, c):
        _chunk_copy(ys_ref, 0, ysort_ref.at[slot], 0, sems.at[slot]).wait()
        return c

    lax.fori_loop(0, lax.fori_loop(0, N_EXPERTS, count, 0), wait, 0)

    lpt = lpt_ref[...]
    gt = gt_ref[...]
    j_iota = lax.broadcasted_iota(jnp.int32, (lpt.shape[0], SORT_ROWS), 1)
    weights = jnp.zeros(j_iota.shape, F32)
    for k in range(TOP_K):
        weights = weights + jnp.where(lpt[:, k:k + 1] == j_iota, gt[:, k:k + 1], 0.0)
    w_hi = weights.astype(BF16)
    w_lo = (weights - w_hi.astype(F32)).astype(BF16)
    y = ysort_ref[slot]
    y_hi = y.astype(BF16)
    y_lo = (y - y_hi.astype(F32)).astype(BF16)
    f = (jnp.dot(w_hi, y_hi, preferred_element_type=F32) + jnp.dot(w_lo, y_hi, preferred_element_type=F32)
         + jnp.dot(w_hi, y_lo, preferred_element_type=F32))
    out_ref[...] = lat_ref[...] + m_ref[0][5:6] * f


def _combine_call(seg3, lpt, ys, gates_t, lat, mods, nt_lat):
    T, D = lat.shape
    nt = T // TILE
    row = lambda i: (i, 0)
    return pl.pallas_call(
        _combine_kernel,
        grid=(nt,),
        in_specs=[
            pl.BlockSpec((1, 1, LANE), lambda i: (i, 0, 0), memory_space=pltpu.SMEM),
            pl.BlockSpec((1, 1, LANE), lambda i: (jnp.minimum(i + 1, nt - 1), 0, 0), memory_space=pltpu.SMEM),
            pl.BlockSpec((TILE, LANE), row),
            pl.BlockSpec((TILE, LANE), row),
            pl.BlockSpec(memory_space=pl.ANY),
            pl.BlockSpec((TILE, D), row),
            pl.BlockSpec((1, 6, D), lambda i: (i // nt_lat, 0, 0)),
        ],
        out_specs=pl.BlockSpec((TILE, D), row),
        out_shape=jax.ShapeDtypeStruct((T, D), F32),
        scratch_shapes=[pltpu.VMEM((2, SORT_ROWS, D), F32), pltpu.SemaphoreType.DMA((2,))],
        compiler_params=_params(("arbitrary",)),
        name="combine",
    )(seg3, seg3, lpt, gates_t, ys, lat, mods)


def _block_diag(blocks):
    H, a, b = blocks.shape
    eye = jnp.eye(H, dtype=blocks.dtype)
    return (eye[:, None, :, None] * blocks[:, :, None, :]).reshape(H * a, H * b)


def _head_slabs(w, width):
    lead = w.shape[:-1]
    w = w.reshape(lead + (MLA_HEADS, width))
    w = jnp.pad(w, [(0, 0)] * len(lead) + [(0, 0), (0, HEAD_PAD - width)])
    return w.reshape(lead + (MLA_HEADS * HEAD_PAD,))


def _rope_tables(rows, n_ctx):
    n_freq = MLA_ROPE // 4
    inv = ROPE_BASE ** (-jnp.arange(n_freq, dtype=F32) / n_freq)
    r = jnp.repeat(jnp.arange(rows, dtype=F32), GRID_W)
    col = jnp.tile(jnp.arange(GRID_W, dtype=F32), rows)
    ang = jnp.concatenate([r[:, None] * inv, col[:, None] * inv], axis=-1)
    cos, sin = jnp.cos(ang), jnp.sin(ang)
    n = cos.shape[0]
    half = MLA_ROPE // 2
    pad = HEAD_PAD - MLA_QK
    c_tab = jnp.concatenate([jnp.ones((n, MLA_NOPE), F32), cos, cos, jnp.ones((n, pad), F32)], axis=1)
    s_tab = jnp.concatenate([jnp.zeros((n, MLA_NOPE), F32), -sin, sin, jnp.zeros((n, pad), F32)], axis=1)
    c_tab = jnp.concatenate([c_tab, jnp.ones((n_ctx, HEAD_PAD), F32)], axis=0)
    s_tab = jnp.concatenate([s_tab, jnp.zeros((n_ctx, HEAD_PAD), F32)], axis=0)
    del half
    return c_tab, s_tab


def kernel(x, c, ctx, c_ctx, w_mod, b_mod, norm_mix, norm_ffn, w_in, w_out, pool_w, pool_scale, mla_q_a_norm, mla_w_uq, mla_kv_a_norm, mla_w_ukv, mla_q_norm, mla_k_norm, sgu_norm_g, sgu_norm_b, sgu_ws, sgu_b, lru_conv_w, lru_conv_b, lru_wa, lru_ba, lru_wx, lru_bx, lru_lambda, router_w, router_b, moe_w_gu, moe_b_gu, moe_w_down, moe_b_down):
    B, N, D = x.shape
    n_ctx = ctx.shape[1]
    L = w_mod.shape[0]
    T = N + n_ctx
    assert B == 1 and N % TILE == 0 and n_ctx == TILE and T % TOK_TILE == 0 and TOK_TILE % SGU_CHUNK == 0
    nt = T // TILE
    nt_lat = N // TILE

    lat = jnp.concatenate([x[0], ctx[0]], axis=0)
    crows = jnp.zeros((8, D), F32).at[0].set(c[0]).at[1].set(c_ctx)
    mods_all = _mod_call(crows, w_mod, b_mod)
    c_tab, s_tab = _rope_tables(N // GRID_W, n_ctx)
    tri = jnp.triu(jnp.ones((TILE, TILE), F32), k=1).astype(BF16)
    ltri = jnp.tril(jnp.ones((N_EXPERTS, N_EXPERTS), F32), k=-1).astype(BF16)

    block_u = MOE_BM // SEG
    n_blocks = -(-(T * TOP_K + nt * N_EXPERTS * (SEG - 1) + N_EXPERTS * (MOE_BM - 1)) // MOE_BM)
    n_rows = n_blocks * MOE_BM

    for l in range(L):
        mods = mods_all[l, :2].reshape(2, 6, D)

        wa_, wqa, wkva, wkr, wc_, wdx, wdg = jnp.split(
            w_in[l], [256, 448, 576, 608, 1120, 1376], axis=1)
        wkr_placed = jnp.pad(wkr.reshape(D, 1, MLA_ROPE),
                             ((0, 0), (0, 0), (MLA_NOPE, HEAD_PAD - MLA_QK)))
        wkr_placed = jnp.tile(wkr_placed, (1, MLA_HEADS, 1)).reshape(D, MLA_HEADS * HEAD_PAD)
        win = jnp.concatenate(
            [wa_, jnp.pad(wqa, ((0, 0), (0, 256 - MLA_Q_RANK))), wkva, wkr_placed, wc_, wdx, wdg],
            axis=1).astype(BF16)
        assert win.shape[1] == IN_COLS_P
        gqa = jnp.pad(mla_q_a_norm[l], (0, 256 - MLA_Q_RANK)).reshape(1, 256)
        wuq = jnp.pad(_head_slabs(mla_w_uq[l], MLA_QK), ((0, 256 - MLA_Q_RANK), (0, 0))).astype(BF16)
        gkva = mla_kv_a_norm[l].reshape(1, MLA_KV_RANK)
        wukv3 = mla_w_ukv[l].reshape(MLA_KV_RANK, MLA_HEADS, MLA_NOPE + MLA_V)
        wuk = _head_slabs(wukv3[:, :, :MLA_NOPE].reshape(MLA_KV_RANK, -1), MLA_NOPE)
        wuv = _head_slabs(wukv3[:, :, MLA_NOPE:].reshape(MLA_KV_RANK, -1), MLA_V)
        wukv = jnp.concatenate([wuk, wuv], axis=1).astype(BF16)
        gq = jnp.pad(mla_q_norm[l], (0, HEAD_PAD - MLA_QK)).reshape(1, HEAD_PAD)
        gk = jnp.pad(mla_k_norm[l], (0, HEAD_PAD - MLA_QK)).reshape(1, HEAD_PAD)
        wscat = sgu_ws[l].transpose(1, 0, 2).reshape(SGU_CHUNK, SGU_HEADS * SGU_CHUNK).astype(BF16)
        sbias = jnp.repeat(sgu_b[l].T, SGU_HD, axis=1)

        pa, q, k, vt, c_s, xd, gd = _inproj_call(
            lat, mods, norm_mix[l].reshape(1, D), win, gqa, wuq, gkva, wukv, gq, gk, c_tab, s_tab,
            sgu_norm_g[l].reshape(1, GROUP_W), sgu_norm_b[l].reshape(1, GROUP_W), wscat, sbias, N)

        wpool = _block_diag(pool_w[l]).astype(BF16)
        wlru = jnp.concatenate(
            [_block_diag(lru_wa[l, 0]), _block_diag(lru_wx[l, 0]),
             _block_diag(lru_wa[l, 1]), _block_diag(lru_wx[l, 1])], axis=1).astype(BF16)
        blru = jnp.concatenate([lru_ba[l, 0], lru_bx[l, 0], lru_ba[l, 1], lru_bx[l, 1]]).reshape(1, -1)
        a_s, hf, ab, bb = _seq_fwd_call(
            pa, xd, wpool, pool_scale[l].reshape(1, GROUP_W), lru_conv_w[l],
            lru_conv_b[l].reshape(1, GROUP_W), wlru, blru, lru_lambda[l], nt_lat, N, n_ctx)
        d_s = _seq_bwd_call(ab, bb, hf, gd)

        tk = next(t for t in (3328, 1280, TILE) if T % t == 0)
        b_lat = _attn_call(q, k, vt, N, 512, 0, T, tk, 0)
        if l < L - 1:
            b_ctx = _attn_call(q, k, vt, n_ctx, n_ctx, N // n_ctx, n_ctx, n_ctx, N // n_ctx)
        else:
            b_ctx = jnp.zeros((n_ctx, MLA_HEADS * MLA_V), BF16)
        b_s = jnp.concatenate([b_lat, b_ctx], axis=0)

        rw_hi = router_w[l].astype(BF16)
        rw_lo = (router_w[l] - rw_hi.astype(F32)).astype(BF16)
        lane_pad = ((0, 0), (0, LANE - N_EXPERTS))
        rw = jnp.concatenate([jnp.pad(rw_hi, lane_pad), jnp.pad(rw_lo, lane_pad)], axis=1)
        lat2, fin, top_e, gates_t = _outproj_call(
            a_s, b_s, c_s, d_s, lat, mods, w_out[l].astype(BF16), norm_ffn[l].reshape(1, D),
            rw, router_b[l].reshape(N_EXPERTS, 1), N)
        lp, lpt, seg, tot = _plan_call(top_e, tri, ltri)

        seg = seg.reshape(nt, N_EXPERTS, LANE)
        tot_u = tot[:, 0].astype(jnp.int32)
        padded_u = (tot_u + block_u - 1) // block_u * block_u
        pad_end_u = jnp.cumsum(padded_u)
        pad_start_u = pad_end_u - padded_u
        seg3 = jnp.concatenate(
            [seg[:, :, 0], seg[:, :, 1], seg[:, :, 2] + pad_start_u[None, :],
             jnp.zeros((nt, LANE - 3 * N_EXPERTS), jnp.int32)], axis=1).reshape(nt, 1, LANE)
        block_u0 = jnp.arange(n_blocks, dtype=jnp.int32) * block_u
        block_e = jnp.minimum(
            jnp.sum((pad_end_u[None, :] <= block_u0[:, None]).astype(jnp.int32), axis=1),
            N_EXPERTS - 1)
        n_used = (pad_end_u[-1:] // block_u).astype(jnp.int32)

        xs = _dispatch_call(pad_start_u + tot_u, pad_end_u, seg3, lp, fin, n_rows)
        ys = _expert_call(block_e, n_used, xs, moe_w_gu, moe_b_gu, moe_w_down, moe_b_down, l)
        lat = _combine_call(seg3, lpt, ys, gates_t, lat2, mods, nt_lat)

    return lat[:N].reshape(B, N, D)
```

```python
import functools

import jax
import jax.numpy as jnp
from jax import lax
from jax.experimental import pallas as pl
from jax.experimental.pallas import tpu as pltpu

F32 = jnp.float32
BF16 = jnp.bfloat16
HIGHEST = lax.Precision.HIGHEST

EPS = 1e-6
LOG2_E = 1.4426950408889634
GRID_W = 64
GROUP_W = 256
POOL_WINDOWS = (2, 4, 8, 16)
POOL_CH = GROUP_W // len(POOL_WINDOWS)
MLA_HEADS = 4
MLA_NOPE = 64
MLA_ROPE = 32
MLA_QK = MLA_NOPE + MLA_ROPE
MLA_V = 64
MLA_Q_RANK = 192
MLA_KV_RANK = 128
ROPE_BASE = 10000.0
SGU_HEADS = 4
SGU_HD = GROUP_W // SGU_HEADS
SGU_CHUNK = 128
LRU_HEADS = 4
LRU_HD = GROUP_W // LRU_HEADS
CONV_W = 4
CONV_LEFT = 1
LRU_C = 8.0
N_EXPERTS = 32
TOP_K = 4
SWIGLU_LIMIT = 7.0
SWIGLU_ALPHA = 1.702

LANE = 128
HEAD_PAD = LANE
TILE = 256
TOK_TILE = 640
HALO = 16
V_ROWS = 80
ATTN_CHUNK = 256
MOE_BM = 512
VMEM_LIMIT = 56 * 1024 * 1024

C_A = 0
C_QA = C_A + GROUP_W
C_KVA = C_QA + 256
C_KR = C_KVA + MLA_KV_RANK
C_CU = C_KR + MLA_HEADS * HEAD_PAD
C_CV = C_CU + GROUP_W
C_DX = C_CV + GROUP_W
C_DG = C_DX + GROUP_W
IN_COLS_P = C_DG + GROUP_W


def _params(sem, vmem=VMEM_LIMIT):
    return pltpu.CompilerParams(dimension_semantics=sem, vmem_limit_bytes=vmem)


def _rms(x, n):
    return x * lax.rsqrt(jnp.sum(x * x, axis=-1, keepdims=True) * (1.0 / n) + EPS)


def _mod_kernel(c_ref, w_ref, b_ref, o_ref):
    cv = c_ref[...]
    s = cv * jax.nn.sigmoid(cv)
    o_ref[0] = jnp.dot(s, w_ref[0], precision=HIGHEST, preferred_element_type=F32) + b_ref[0]


def _mod_call(crows, w_mod, b_mod):
    L, D, M = w_mod.shape
    bn = 1536
    return pl.pallas_call(
        _mod_kernel,
        grid=(L, M // bn),
        in_specs=[
            pl.BlockSpec((8, D), lambda l, j: (0, 0)),
            pl.BlockSpec((1, D, bn), lambda l, j: (l, 0, j)),
            pl.BlockSpec((1, 1, bn), lambda l, j: (l, 0, j)),
        ],
        out_specs=pl.BlockSpec((1, 8, bn), lambda l, j: (l, 0, j)),
        out_shape=jax.ShapeDtypeStruct((L, 8, M), F32),
        compiler_params=_params(("arbitrary", "arbitrary")),
        name="mod",
    )(crows, w_mod, b_mod.reshape(L, 1, M))


def _rope_heads(x, gain, c_tab, s_tab, lane, scale):
    outs = []
    for h in range(MLA_HEADS):
        xh = x[:, h * HEAD_PAD:(h + 1) * HEAD_PAD]
        xn = _rms(xh, MLA_QK) * gain
        swap = jnp.where(lane < MLA_NOPE + MLA_ROPE // 2,
                         pltpu.roll(xn, HEAD_PAD - MLA_ROPE // 2, axis=1),
                         pltpu.roll(xn, MLA_ROPE // 2, axis=1))
        r = xn * c_tab + swap * s_tab
        if scale != 1.0:
            r = r * scale
        outs.append(r.astype(BF16))
    return jnp.concatenate(outs, axis=1)


def _mod_row(m_ref, is_ctx, k):
    return jnp.where(is_ctx, m_ref[1, k:k + 1, :], m_ref[0, k:k + 1, :])


def _is_ctx_rows(tm, n_lat):
    row = pl.program_id(0) * tm + lax.broadcasted_iota(jnp.int32, (tm, 1), 0)
    return row >= n_lat


def _inproj_kernel(n_lat, x_ref, m_ref, gn_ref, win_ref, gqa_ref, wuq_ref, gkva_ref, wukv_ref,
                   gq_ref, gk_ref, ct_ref, st_ref, sg_ref, sb_ref, ws_ref, sbias_ref,
                   pa_ref, q_ref, k_ref, vt_ref, c_ref, xd_ref, gd_ref):
    x = x_ref[...]
    is_ctx = _is_ctx_rows(x.shape[0], n_lat)
    h = _rms(x, x.shape[-1]) * gn_ref[...]
    h = h * (1.0 + _mod_row(m_ref, is_ctx, 1)) + _mod_row(m_ref, is_ctx, 0)
    p = jnp.dot(h.astype(BF16), win_ref[...], preferred_element_type=F32)

    pa_ref[...] = p[:, C_A:C_A + GROUP_W]
    xd_ref[...] = p[:, C_DX:C_DX + GROUP_W]
    gd_ref[...] = jax.nn.gelu(p[:, C_DG:C_DG + GROUP_W])

    lane = lax.broadcasted_iota(jnp.int32, (x.shape[0], HEAD_PAD), 1)
    c_tab = ct_ref[...]
    s_tab = st_ref[...]
    qa = _rms(p[:, C_QA:C_QA + 256], MLA_Q_RANK) * gqa_ref[...]
    q = jnp.dot(qa.astype(BF16), wuq_ref[...], preferred_element_type=F32)
    q_ref[...] = _rope_heads(q, gq_ref[...], c_tab, s_tab, lane, MLA_QK ** -0.5 * LOG2_E)

    kva = _rms(p[:, C_KVA:C_KVA + MLA_KV_RANK], MLA_KV_RANK) * gkva_ref[...]
    kv = jnp.dot(kva.astype(BF16), wukv_ref[...], preferred_element_type=F32)
    kpre = kv[:, :MLA_HEADS * HEAD_PAD] + p[:, C_KR:C_KR + MLA_HEADS * HEAD_PAD]
    k_ref[...] = _rope_heads(kpre, gk_ref[...], c_tab, s_tab, lane, 1.0)
    v = kv[:, MLA_HEADS * HEAD_PAD:]
    slab_lane = lax.broadcasted_iota(jnp.int32, v.shape, 1) % HEAD_PAD
    vt_ref[...] = jnp.where(slab_lane == MLA_V, 1.0, v).T.astype(BF16)

    z = jax.nn.gelu(p[:, C_CU:C_CU + 2 * GROUP_W])
    u = z[:, :GROUP_W]
    v = z[:, GROUP_W:]
    mu = jnp.mean(v, axis=-1, keepdims=True)
    vc = v - mu
    var = jnp.mean(vc * vc, axis=-1, keepdims=True)
    vn = (vc * lax.rsqrt(var + EPS) * sg_ref[...] + sb_ref[...]).astype(BF16)
    head_of_lane = lax.broadcasted_iota(jnp.int32, (SGU_CHUNK, GROUP_W), 1) // SGU_HD
    zero = jnp.zeros((SGU_CHUNK, GROUP_W), BF16)
    for cch in range(x.shape[0] // SGU_CHUNK):
        rows = slice(cch * SGU_CHUNK, (cch + 1) * SGU_CHUNK)
        vch = vn[rows]
        stacked = jnp.concatenate(
            [jnp.where(head_of_lane == hh, vch, zero) for hh in range(SGU_HEADS)], axis=0)
        mixed = jnp.dot(ws_ref[...], stacked, preferred_element_type=F32) + sbias_ref[...]
        c_ref[rows, :] = (u[rows] * mixed).astype(BF16)


def _inproj_call(xcat, mods, gn, win, gqa, wuq, gkva, wukv, gq, gk, ctab, stab, sg, sb, wscat, sbias,
                 n_lat):
    T, D = xcat.shape
    tm = TILE
    nt = T // tm
    row = lambda i: (i, 0)

    def full(a):
        return pl.BlockSpec(a.shape, lambda i: (0,) * a.ndim)

    out_shapes = (
        jax.ShapeDtypeStruct((T, GROUP_W), F32),
        jax.ShapeDtypeStruct((T, MLA_HEADS * HEAD_PAD), BF16),
        jax.ShapeDtypeStruct((T, MLA_HEADS * HEAD_PAD), BF16),
        jax.ShapeDtypeStruct((MLA_HEADS * HEAD_PAD, T), BF16),
        jax.ShapeDtypeStruct((T, GROUP_W), BF16),
        jax.ShapeDtypeStruct((T, GROUP_W), F32),
        jax.ShapeDtypeStruct((T, GROUP_W), F32),
    )
    out_specs = [pl.BlockSpec((tm, s.shape[1]), row) for s in out_shapes]
    out_specs[3] = pl.BlockSpec((MLA_HEADS * HEAD_PAD, tm), lambda i: (0, i))
    return pl.pallas_call(
        functools.partial(_inproj_kernel, n_lat),
        grid=(nt,),
        in_specs=[
            pl.BlockSpec((tm, D), row),
            full(mods),
            full(gn), full(win), full(gqa), full(wuq), full(gkva), full(wukv), full(gq), full(gk),
            pl.BlockSpec((tm, HEAD_PAD), row),
            pl.BlockSpec((tm, HEAD_PAD), row),
            full(sg), full(sb), full(wscat), full(sbias),
        ],
        out_specs=tuple(out_specs),
        out_shape=out_shapes,
        compiler_params=_params(("parallel",)),
        name="inproj",
    )(xcat, mods, gn, win, gqa, wuq, gkva, wukv, gq, gk, ctab, stab, sg, sb, wscat, sbias)


def _shift_rows(x, d, fill, reverse):
    n = x.shape[0]
    t = lax.broadcasted_iota(jnp.int32, x.shape, 0)
    if reverse:
        return jnp.where(t < n - d, pltpu.roll(x, n - d, axis=0), fill)
    return jnp.where(t >= d, pltpu.roll(x, d, axis=0), fill)


def _tile_scan(a, b, reverse):
    d = 1
    while d < a.shape[0]:
        a_s = _shift_rows(a, d, 1.0, reverse)
        b_s = _shift_rows(b, d, 0.0, reverse)
        b = a * b_s + b
        a = a * a_s
        d *= 2
    return a, b


def _lru_coeffs(xc, proj, sp, d):
    r = jax.nn.sigmoid(proj[:, (2 * d) * GROUP_W:(2 * d + 1) * GROUP_W])
    i = jax.nn.sigmoid(proj[:, (2 * d + 1) * GROUP_W:(2 * d + 2) * GROUP_W])
    log_a = -LRU_C * r * sp[d:d + 1]
    a = jnp.exp(log_a)
    drive = jnp.sqrt(1.0 - jnp.exp(2.0 * log_a)) * (i * xc)
    return a, drive


def _seq_fwd_kernel(nt_lat, n_lat, n_ctx,
                    pa_ref, pa_prev_ref, pa_next_ref, xd_ref, xd_prev_ref, xd_next_ref,
                    wpool_ref, pscale_ref, cw_ref, cb_ref, wlru_ref, blru_ref, lam_ref,
                    a_out_ref, hf_ref, ab_ref, bb_ref,
                    ext_ref, carry_ref):
    j = pl.program_id(0)
    nt = pl.num_programs(0)
    ti = (j + nt_lat) % nt
    is_ctx = ti >= nt_lat
    seq_first = jnp.logical_or(ti == 0, ti == nt_lat)
    seq_last = jnp.logical_or(ti == nt_lat - 1, ti == nt - 1)
    t_loc = (ti - jnp.where(is_ctx, nt_lat, 0)) * TILE
    n_seq = jnp.where(is_ctx, n_ctx, n_lat)

    @pl.when(j == 0)
    def _():
        carry_ref[...] = jnp.zeros_like(carry_ref)

    def load_ext(cur_ref, prev_ref, next_ref):
        ext_ref[0:HALO, :] = jnp.where(seq_first, 0.0, prev_ref[...])
        ext_ref[HALO:HALO + TILE, :] = cur_ref[...]
        ext_ref[HALO + TILE:, :] = jnp.where(seq_last, 0.0, next_ref[...])

    def win(off):
        return ext_ref[pl.ds(HALO + off, TILE), :]

    load_ext(pa_ref, pa_prev_ref, pa_next_ref)
    x = pa_ref[...]
    t = t_loc + lax.broadcasted_iota(jnp.int32, (TILE, GROUP_W), 0)
    lane = lax.broadcasted_iota(jnp.int32, (TILE, GROUP_W), 1)
    mean = jnp.zeros((TILE, GROUP_W), F32)
    for g, w in enumerate(POOL_WINDOWS):
        acc = win(-(w // 2))
        for off in range(-(w // 2) + 1, w // 2):
            acc = acc + win(off)
        cnt = jnp.minimum(t + w // 2, n_seq) - jnp.maximum(t - w // 2, 0)
        mean = jnp.where(lane // POOL_CH == g, acc / cnt.astype(F32), mean)
    diff = (mean - x).astype(BF16)
    pooled = jnp.dot(diff, wpool_ref[...], preferred_element_type=F32) * pscale_ref[...]
    a_out_ref[...] = pooled.astype(BF16)

    load_ext(xd_ref, xd_prev_ref, xd_next_ref)
    xc = jnp.zeros((TILE, GROUP_W), F32) + cb_ref[...]
    for k in range(CONV_W):
        xc = xc + win(k - CONV_LEFT) * cw_ref[k:k + 1, :]

    proj = jnp.dot(xc.astype(BF16), wlru_ref[...], preferred_element_type=F32) + blru_ref[...]
    lam = lam_ref[...]
    sp = jnp.maximum(-lam, 0.0) + jnp.log(1.0 + jnp.exp(-jnp.abs(lam)))
    a_f, b_f = _lru_coeffs(xc, proj, sp, 0)
    a_b, b_b = _lru_coeffs(xc, proj, sp, 1)
    ab_ref[...] = a_b
    bb_ref[...] = b_b

    big_a, big_b = _tile_scan(a_f, b_f, reverse=False)
    hf = big_b + big_a * carry_ref[0:1, :]
    hf_ref[...] = hf
    carry_ref[0:1, :] = hf[TILE - 1:TILE, :]


def _seq_fwd_call(pa, xd, wpool, pscale, cw, cb, wlru, blru, lam, nt_lat, n_lat, n_ctx):
    T = pa.shape[0]
    nt = T // TILE
    hb = TILE // HALO
    n_halo = T // HALO

    def tile_of(j):
        return (j + nt_lat) % nt

    cur = lambda j: (tile_of(j), 0)
    prev = lambda j: (jnp.maximum(tile_of(j) * hb - 1, 0), 0)
    nxt = lambda j: (jnp.minimum((tile_of(j) + 1) * hb, n_halo - 1), 0)
    fixed = lambda j: (0, 0)

    def full(a):
        return pl.BlockSpec(a.shape, fixed)

    tile_spec = pl.BlockSpec((TILE, GROUP_W), cur)
    out_shapes = (
        jax.ShapeDtypeStruct((T, GROUP_W), BF16),
        jax.ShapeDtypeStruct((T, GROUP_W), F32),
        jax.ShapeDtypeStruct((T, GROUP_W), F32),
        jax.ShapeDtypeStruct((T, GROUP_W), F32),
    )
    return pl.pallas_call(
        functools.partial(_seq_fwd_kernel, nt_lat, n_lat, n_ctx),
        grid=(nt,),
        in_specs=[
            tile_spec, pl.BlockSpec((HALO, GROUP_W), prev), pl.BlockSpec((HALO, GROUP_W), nxt),
            tile_spec, pl.BlockSpec((HALO, GROUP_W), prev), pl.BlockSpec((HALO, GROUP_W), nxt),
            full(wpool), full(pscale), full(cw), full(cb), full(wlru), full(blru), full(lam),
        ],
        out_specs=tuple(pl.BlockSpec((TILE, GROUP_W), cur) for _ in out_shapes),
        out_shape=out_shapes,
        scratch_shapes=[pltpu.VMEM((TILE + 2 * HALO, GROUP_W), F32), pltpu.VMEM((8, GROUP_W), F32)],
        compiler_params=_params(("arbitrary",)),
        name="seq_fwd",
    )(pa, pa, pa, xd, xd, xd, wpool, pscale, cw, cb, wlru, blru, lam)


def _seq_bwd_kernel(ab_ref, bb_ref, hf_ref, gd_ref, d_out_ref, carry_ref):
    @pl.when(pl.program_id(0) == 0)
    def _():
        carry_ref[...] = jnp.zeros_like(carry_ref)

    big_a, big_b = _tile_scan(ab_ref[...], bb_ref[...], reverse=True)
    hb = big_b + big_a * carry_ref[0:1, :]
    carry_ref[0:1, :] = hb[0:1, :]
    d_out_ref[...] = (gd_ref[...] * (hf_ref[...] + hb)).astype(BF16)


def _seq_bwd_call(ab, bb, hf, gd):
    T = ab.shape[0]
    nt = T // TILE
    spec = pl.BlockSpec((TILE, GROUP_W), lambda j: (nt - 1 - j, 0))
    return pl.pallas_call(
        _seq_bwd_kernel,
        grid=(nt,),
        in_specs=[spec, spec, spec, spec],
        out_specs=spec,
        out_shape=jax.ShapeDtypeStruct((T, GROUP_W), BF16),
        scratch_shapes=[pltpu.VMEM((8, GROUP_W), F32)],
        compiler_params=_params(("arbitrary",)),
        name="seq_bwd",
    )(ab, bb, hf, gd)


def _attn_kernel(q_ref, k_ref, vt_ref, o_ref, m_ref, acc_ref):
    kv = pl.program_id(1)

    @pl.when(kv == 0)
    def _():
        m_ref[...] = jnp.full_like(m_ref, -jnp.inf)
        acc_ref[...] = jnp.zeros_like(acc_ref)

    def head_cols(h):
        return slice(h * HEAD_PAD, (h + 1) * HEAD_PAD)

    ck = ATTN_CHUNK if k_ref.shape[0] % ATTN_CHUNK == 0 else k_ref.shape[0]
    items = [(c, h) for c in range(k_ref.shape[0] // ck) for h in range(MLA_HEADS)]

    def scores(c, h):
        cols = head_cols(h)
        return lax.dot_general(k_ref[c * ck:(c + 1) * ck, cols], q_ref[:, cols], (((1,), (1,)), ((), ())),
                               preferred_element_type=F32)

    def probs(h, s):
        m_prev = m_ref[h]
        m_new = jnp.maximum(m_prev, jnp.max(s, axis=0, keepdims=True))
        m_ref[h] = m_new
        return jnp.exp2(m_prev - m_new), jnp.exp2(s - m_new).astype(BF16)

    def accumulate(c, h, alpha, p):
        rows = slice(h * HEAD_PAD, h * HEAD_PAD + V_ROWS)
        acc_ref[h, :V_ROWS] = alpha * acc_ref[h, :V_ROWS] + jnp.dot(vt_ref[rows, c * ck:(c + 1) * ck], p,
                                                                    preferred_element_type=F32)

    s_next = scores(*items[0])
    pending = None
    for n, (c, h) in enumerate(items):
        s_cur = s_next
        if n + 1 < len(items):
            s_next = scores(*items[n + 1])
        if pending is not None:
            accumulate(*pending)
        pending = (c, h) + probs(h, s_cur)
    accumulate(*pending)

    @pl.when(kv == pl.num_programs(1) - 1)
    def _():
        outs = []
        for h in range(MLA_HEADS):
            a = acc_ref[h]
            outs.append((a / a[MLA_V:MLA_V + 1, :]).T[:, :MLA_V])
        o_ref[...] = jnp.concatenate(outs, axis=1).astype(BF16)


def _attn_call(q, k, vt, n_q, tq, q_blk0, n_k, tk, k_blk0):
    width = MLA_HEADS * HEAD_PAD
    return pl.pallas_call(
        _attn_kernel,
        grid=(n_q // tq, n_k // tk),
        in_specs=[
            pl.BlockSpec((tq, width), lambda i, j: (q_blk0 + i, 0)),
            pl.BlockSpec((tk, width), lambda i, j: (k_blk0 + j, 0)),
            pl.BlockSpec((width, tk), lambda i, j: (0, k_blk0 + j)),
        ],
        out_specs=pl.BlockSpec((tq, MLA_HEADS * MLA_V), lambda i, j: (i, 0)),
        out_shape=jax.ShapeDtypeStruct((n_q, MLA_HEADS * MLA_V), BF16),
        scratch_shapes=[
            pltpu.VMEM((MLA_HEADS, 1, tq), F32),
            pltpu.VMEM((MLA_HEADS, HEAD_PAD, tq), F32),
        ],
        compiler_params=_params(("parallel", "arbitrary")),
        name="attn",
    )(q, k, vt)


def _outproj_kernel(n_lat, a_ref, b_ref, c_ref, d_ref, x_ref, m_ref, wout_ref, gn_ref, rw_ref, rb_ref,
                    lat_ref, fin_ref, tope_ref, gatet_ref):
    tm = x_ref.shape[0]
    n_e = rb_ref.shape[0]
    is_ctx = _is_ctx_rows(tm, n_lat)
    y = jnp.dot(jnp.concatenate([a_ref[...], b_ref[...], c_ref[...], d_ref[...]], axis=1),
                wout_ref[...], preferred_element_type=F32)
    lat = x_ref[...] + _mod_row(m_ref, is_ctx, 2) * y
    lat_ref[...] = lat
    f = _rms(lat, lat.shape[-1]) * gn_ref[...]
    f = f * (1.0 + _mod_row(m_ref, is_ctx, 4)) + _mod_row(m_ref, is_ctx, 3)

    f_hi = f.astype(BF16)
    fin_ref[...] = f_hi
    f_lo = (f - f_hi.astype(F32)).astype(BF16)
    hi_prod = jnp.dot(f_hi, rw_ref[...], preferred_element_type=F32)
    lo_prod = jnp.dot(f_lo, rw_ref[:, :LANE], preferred_element_type=F32)
    logits_rows = hi_prod[:, :LANE] + hi_prod[:, LANE:] + lo_prod
    logits = logits_rows.T[:n_e] + rb_ref[...]
    e_iota = lax.broadcasted_iota(jnp.int32, (n_e, tm), 0)
    vals, idxs = [], []
    for _ in range(TOP_K):
        mx = jnp.max(logits, axis=0, keepdims=True)
        idx = jnp.min(jnp.where(logits == mx, e_iota, n_e), axis=0, keepdims=True)
        vals.append(mx)
        idxs.append(idx)
        logits = jnp.where(e_iota == idx, -jnp.inf, logits)
    ex = [jnp.exp(vv - vals[0]) for vv in vals]
    den = ex[0] + ex[1] + ex[2] + ex[3]
    row8 = lax.broadcasted_iota(jnp.int32, (8, tm), 0)
    row128 = lax.broadcasted_iota(jnp.int32, (LANE, tm), 0)
    tope = jnp.zeros((8, tm), jnp.int32)
    gates = jnp.zeros((LANE, tm), F32)
    for k in range(TOP_K):
        tope = jnp.where(row8 == k, idxs[k], tope)
        gates = jnp.where(row128 == k, ex[k] / den, gates)
    tope_ref[...] = tope
    gatet_ref[...] = gates.T


def _outproj_call(a, b, c, d, xcat, mods, wout, gn, rw, rb, n_lat):
    T, D = xcat.shape
    tm = TOK_TILE
    row = lambda i: (i, 0)

    def full(arr):
        return pl.BlockSpec(arr.shape, lambda i: (0,) * arr.ndim)

    g_spec = pl.BlockSpec((tm, GROUP_W), row)
    out_shapes = (
        jax.ShapeDtypeStruct((T, D), F32),
        jax.ShapeDtypeStruct((T, D), BF16),
        jax.ShapeDtypeStruct((8, T), jnp.int32),
        jax.ShapeDtypeStruct((T, LANE), F32),
    )
    out_specs = (
        pl.BlockSpec((tm, D), row), pl.BlockSpec((tm, D), row),
        pl.BlockSpec((8, tm), lambda i: (0, i)),
        pl.BlockSpec((tm, LANE), row),
    )
    return pl.pallas_call(
        functools.partial(_outproj_kernel, n_lat),
        grid=(T // tm,),
        in_specs=[
            g_spec, g_spec, g_spec, g_spec,
            pl.BlockSpec((tm, D), row),
            full(mods), full(wout), full(gn), full(rw), full(rb),
        ],
        out_specs=out_specs,
        out_shape=out_shapes,
        compiler_params=_params(("parallel",)),
        name="outproj",
    )(a, b, c, d, xcat, mods, wout, gn, rw, rb)


SEG = 8
SORT_ROWS = TOP_K * TILE + N_EXPERTS * SEG
SEG_START, SEG_COUNT, SEG_DEST = 0, N_EXPERTS, 2 * N_EXPERTS


def _plan_kernel(tope_ref, tri_ref, ltri_ref, lp_ref, lpt_ref, seg_ref, tot_ref, carry_ref):
    @pl.when(pl.program_id(0) == 0)
    def _():
        carry_ref[...] = jnp.zeros_like(carry_ref)

    tope = tope_ref[...]
    n_e = ltri_ref.shape[0]
    tm = tope.shape[1]
    e_iota = lax.broadcasted_iota(jnp.int32, (n_e, tm), 0)
    onehot = jnp.concatenate(
        [jnp.where(e_iota == tope[k:k + 1, :], 1.0, 0.0) for k in range(TOP_K)], axis=0)
    prefix = jnp.dot(onehot.astype(BF16), tri_ref[...], preferred_element_type=F32)
    cnt_k = [jnp.sum(onehot[k * n_e:(k + 1) * n_e], axis=1, keepdims=True) for k in range(TOP_K)]
    cnt = cnt_k[0] + cnt_k[1] + cnt_k[2] + cnt_k[3]
    units = jnp.floor((cnt + (SEG - 1)) * (1.0 / SEG))
    units_b = jnp.broadcast_to(units, (n_e, LANE))
    start_u = jnp.dot(ltri_ref[...], units_b.astype(BF16), preferred_element_type=F32)
    base = start_u[:, 0:1] * SEG
    row8 = lax.broadcasted_iota(jnp.int32, (8, tm), 0)
    row128 = lax.broadcasted_iota(jnp.int32, (LANE, tm), 0)
    lp8 = jnp.zeros((8, tm), F32)
    lp128 = jnp.zeros((LANE, tm), F32)
    for k in range(TOP_K):
        hot = onehot[k * n_e:(k + 1) * n_e]
        lp_k = jnp.sum(hot * (prefix[k * n_e:(k + 1) * n_e] + base), axis=0, keepdims=True)
        lp8 = jnp.where(row8 == k, lp_k, lp8)
        lp128 = jnp.where(row128 == k, lp_k, lp128)
        base = base + cnt_k[k]
    lp_ref[...] = lp8.astype(jnp.int32)
    lpt_ref[...] = lp128.T.astype(jnp.int32)
    lane = lax.broadcasted_iota(jnp.int32, (n_e, LANE), 1)
    carry = carry_ref[...]
    seg = jnp.where(lane == 0, start_u, jnp.where(lane == 1, units_b, jnp.where(lane == 2, carry, 0.0)))
    seg_ref[...] = seg.astype(jnp.int32)
    carry_ref[...] = carry + units_b
    tot_ref[...] = carry + units_b


def _plan_call(top_e, tri, ltri):
    T = top_e.shape[1]
    nt = T // TILE
    n_e = ltri.shape[0]
    fixed = lambda i: (0, 0)
    out_shapes = (
        jax.ShapeDtypeStruct((8, T), jnp.int32),
        jax.ShapeDtypeStruct((T, LANE), jnp.int32),
        jax.ShapeDtypeStruct((nt * n_e, LANE), jnp.int32),
        jax.ShapeDtypeStruct((n_e, LANE), F32),
    )
    return pl.pallas_call(
        _plan_kernel,
        grid=(nt,),
        in_specs=[pl.BlockSpec((8, TILE), lambda i: (0, i)), pl.BlockSpec(tri.shape, fixed),
                  pl.BlockSpec(ltri.shape, fixed)],
        out_specs=(pl.BlockSpec((8, TILE), lambda i: (0, i)), pl.BlockSpec((TILE, LANE), lambda i: (i, 0)),
                   pl.BlockSpec((n_e, LANE), lambda i: (i, 0)), pl.BlockSpec((n_e, LANE), fixed)),
        out_shape=out_shapes,
        scratch_shapes=[pltpu.VMEM((n_e, LANE), F32)],
        compiler_params=_params(("arbitrary",)),
        name="plan",
    )(top_e, tri, ltri)


def _chunk_copy(src, s_row, dst, d_row, sem):
    return pltpu.make_async_copy(src.at[pl.ds(pl.multiple_of(s_row, SEG), SEG)],
                                 dst.at[pl.ds(pl.multiple_of(d_row, SEG), SEG)], sem)


def _for_each_chunk(seg_ref, fn):
    def per_expert(e, total):
        n = seg_ref[0, 0, SEG_COUNT + e]
        local = seg_ref[0, 0, SEG_START + e] * SEG
        dest = seg_ref[0, 0, SEG_DEST + e] * SEG

        def per_chunk(j, carry):
            fn(local + j * SEG, dest + j * SEG)
            return carry

        lax.fori_loop(0, n, per_chunk, 0)
        return total + n

    return lax.fori_loop(0, N_EXPERTS, per_expert, 0)


def _dispatch_kernel(lo_ref, hi_ref, seg_ref, lp_ref, fin_ref, xs_ref, sorted_ref, zero_ref, cnt_ref, sems, zero_sem,
                     free_sem):
    i = pl.program_id(0)
    slot = i % 2

    @pl.when(i == 0)
    def _():
        cnt_ref[0] = 0
        cnt_ref[1] = 0
        zero_ref[...] = jnp.zeros_like(zero_ref)

        def per_expert(e, carry):
            def start(u, c):
                _chunk_copy(zero_ref, 0, xs_ref, u * SEG, zero_sem).start()
                return c

            def wait(u, c):
                _chunk_copy(zero_ref, 0, xs_ref, 0, zero_sem).wait()
                return c

            lax.fori_loop(lo_ref[e], hi_ref[e], start, 0)
            lax.fori_loop(lo_ref[e], hi_ref[e], wait, 0)
            return carry

        lax.fori_loop(0, N_EXPERTS, per_expert, 0)

    first_free = hi_ref[N_EXPERTS - 1] // (MOE_BM // SEG)
    n_blocks = xs_ref.shape[0] // MOE_BM

    def free_block_copy(b):
        return pltpu.make_async_copy(zero_ref, xs_ref.at[pl.ds(pl.multiple_of(b * MOE_BM, MOE_BM), MOE_BM)], free_sem)

    @pl.when(i == 0)
    def _():
        def start(b, c):
            free_block_copy(b).start()
            return c

        lax.fori_loop(first_free, n_blocks, start, 0)

    def drain(s):
        def wait(j, c):
            _chunk_copy(sorted_ref.at[s], 0, xs_ref, 0, sems.at[s]).wait()
            return c

        lax.fori_loop(0, cnt_ref[s], wait, 0)

    drain(slot)
    lp = lp_ref[...]
    j_iota = lax.broadcasted_iota(jnp.int32, (SORT_ROWS, lp.shape[1]), 0)
    perm = jnp.zeros(j_iota.shape, F32)
    for k in range(TOP_K):
        perm = jnp.where(j_iota == lp[k:k + 1, :], 1.0, perm)
    sorted_ref[slot] = jnp.dot(perm.astype(BF16), fin_ref[...], preferred_element_type=F32)

    def send(local, dest):
        _chunk_copy(sorted_ref.at[slot], local, xs_ref, dest, sems.at[slot]).start()

    cnt_ref[slot] = _for_each_chunk(seg_ref, send)

    @pl.when(i == pl.num_programs(0) - 1)
    def _():
        drain(slot)
        drain(1 - slot)

        def wait(b, c):
            free_block_copy(b).wait()
            return c

        lax.fori_loop(first_free, n_blocks, wait, 0)


def _dispatch_call(pad_lo, pad_hi, seg3, lp, fin, n_rows):
    nt = seg3.shape[0]
    D = fin.shape[1]
    grid_spec = pltpu.PrefetchScalarGridSpec(
        num_scalar_prefetch=2,
        grid=(nt,),
        in_specs=[
            pl.BlockSpec((1, 1, LANE), lambda i, lo, hi: (i, 0, 0), memory_space=pltpu.SMEM),
            pl.BlockSpec((8, TILE), lambda i, lo, hi: (0, i)),
            pl.BlockSpec((TILE, D), lambda i, lo, hi: (i, 0)),
        ],
        out_specs=pl.BlockSpec(memory_space=pl.ANY),
        scratch_shapes=[pltpu.VMEM((2, SORT_ROWS, D), F32), pltpu.VMEM((MOE_BM, D), F32), pltpu.SMEM((2,), jnp.int32),
                        pltpu.SemaphoreType.DMA((2,)), pltpu.SemaphoreType.DMA(()), pltpu.SemaphoreType.DMA(())],
    )
    return pl.pallas_call(
        _dispatch_kernel,
        grid_spec=grid_spec,
        out_shape=jax.ShapeDtypeStruct((n_rows, D), F32),
        compiler_params=_params(("arbitrary",)),
        name="dispatch",
    )(pad_lo, pad_hi, seg3, lp, fin)


def _expert_kernel(be_ref, nu_ref, xs_ref, wgu_ref, bgu_ref, wdn_ref, bdn_ref, ys_ref, wgu_bf, wdn_bf):
    i = pl.program_id(0)
    changed = jnp.logical_or(i == 0, be_ref[i] != be_ref[jnp.maximum(i - 1, 0)])

    @pl.when(changed)
    def _():
        wgu_bf[...] = wgu_ref[...].astype(BF16)
        wdn_bf[...] = wdn_ref[...].astype(BF16)

    @pl.when(i < nu_ref[0])
    def _():
        f = wdn_bf.shape[0]
        gu = jnp.dot(xs_ref[...].astype(BF16), wgu_bf[...], preferred_element_type=F32) + bgu_ref[...]
        g = jnp.minimum(gu[:, :f], SWIGLU_LIMIT)
        u = jnp.clip(gu[:, f:], -SWIGLU_LIMIT, SWIGLU_LIMIT)
        act = (u + 1.0) * (g * jax.nn.sigmoid(SWIGLU_ALPHA * g))
        ys_ref[...] = jnp.dot(act.astype(BF16), wdn_bf[...], preferred_element_type=F32) + bdn_ref[...]

    @pl.when(i >= nu_ref[0])
    def _():
        ys_ref[...] = jnp.zeros_like(ys_ref)


def _expert_call(block_e, n_used, xs, w_gu, b_gu, w_down, b_down, layer):
    n_rows, D = xs.shape
    _, E, _, F2 = w_gu.shape
    F = F2 // 2
    grid_spec = pltpu.PrefetchScalarGridSpec(
        num_scalar_prefetch=2,
        grid=(n_rows // MOE_BM,),
        in_specs=[
            pl.BlockSpec((MOE_BM, D), lambda i, be, nu: (jnp.minimum(i, nu[0] - 1), 0)),
            pl.BlockSpec((None, None, D, F2), lambda i, be, nu: (layer, be[i], 0, 0)),
            pl.BlockSpec((None, None, 1, F2), lambda i, be, nu: (layer, be[i], 0, 0)),
            pl.BlockSpec((None, None, F, D), lambda i, be, nu: (layer, be[i], 0, 0)),
            pl.BlockSpec((None, None, 1, D), lambda i, be, nu: (layer, be[i], 0, 0)),
        ],
        out_specs=pl.BlockSpec((MOE_BM, D), lambda i, be, nu: (i, 0)),
        scratch_shapes=[pltpu.VMEM((D, F2), BF16), pltpu.VMEM((F, D), BF16)],
    )
    L = w_gu.shape[0]
    return pl.pallas_call(
        _expert_kernel,
        grid_spec=grid_spec,
        out_shape=jax.ShapeDtypeStruct((n_rows, D), F32),
        compiler_params=_params(("arbitrary",)),
        name="expert",
    )(block_e, n_used, xs, w_gu, b_gu.reshape(L, E, 1, F2), w_down, b_down.reshape(L, E, 1, D))


def _combine_kernel(seg_ref, seg_next_ref, lpt_ref, gt_ref, ys_ref, lat_ref, m_ref, out_ref, ysort_ref, sems):
    i = pl.program_id(0)
    slot = i % 2

    def fetch(seg, s):
        def recv(local, src):
            _chunk_copy(ys_ref, src, ysort_ref.at[s], local, sems.at[s]).start()

        _for_each_chunk(seg, recv)

    @pl.when(i == 0)
    def _():
        ysort_ref[...] = jnp.zeros_like(ysort_ref)
        fetch(seg_ref, 0)

    @pl.when(i + 1 < pl.num_programs(0))
    def _():
        fetch(seg_next_ref, 1 - slot)

    def count(e, total):
        return total + seg_ref[0, 0, SEG_COUNT + e]

    def wait(j, c):
        _chunk_copy(ys_ref, 0, ysort_ref.at[slot], 0, sems.at[slot]).wait()
        return c

    lax.fori_loop(0, lax.fori_loop(0, N_EXPERTS, count, 0), wait, 0)

    lpt = lpt_ref[...]
    gt = gt_ref[...]
    j_iota = lax.broadcasted_iota(jnp.int32, (lpt.shape[0], SORT_ROWS), 1)
    weights = jnp.zeros(j_iota.shape, F32)
    for k in range(TOP_K):
        weights = weights + jnp.where(lpt[:, k:k + 1] == j_iota, gt[:, k:k + 1], 0.0)
    w_hi = weights.astype(BF16)
    w_lo = (weights - w_hi.astype(F32)).astype(BF16)
    y = ysort_ref[slot]
    y_hi = y.astype(BF16)
    y_lo = (y - y_hi.astype(F32)).astype(BF16)
    f = (jnp.dot(w_hi, y_hi, preferred_element_type=F32) + jnp.dot(w_lo, y_hi, preferred_element_type=F32)
         + jnp.dot(w_hi, y_lo, preferred_element_type=F32))
    out_ref[...] = lat_ref[...] + m_ref[0][5:6] * f


def _combine_call(seg3, lpt, ys, gates_t, lat, mods, nt_lat, n_out):
    D = lat.shape[1]
    nt = seg3.shape[0]
    row = lambda i: (i, 0)
    return pl.pallas_call(
        _combine_kernel,
        grid=(n_out // TILE,),
        in_specs=[
            pl.BlockSpec((1, 1, LANE), lambda i: (i, 0, 0), memory_space=pltpu.SMEM),
            pl.BlockSpec((1, 1, LANE), lambda i: (jnp.minimum(i + 1, nt - 1), 0, 0), memory_space=pltpu.SMEM),
            pl.BlockSpec((TILE, LANE), row),
            pl.BlockSpec((TILE, LANE), row),
            pl.BlockSpec(memory_space=pl.ANY),
            pl.BlockSpec((TILE, D), row),
            pl.BlockSpec((1, 6, D), lambda i: (i // nt_lat, 0, 0)),
        ],
        out_specs=pl.BlockSpec((TILE, D), row),
        out_shape=jax.ShapeDtypeStruct((n_out, D), F32),
        scratch_shapes=[pltpu.VMEM((2, SORT_ROWS, D), F32), pltpu.SemaphoreType.DMA((2,))],
        compiler_params=_params(("arbitrary",)),
        name="combine",
    )(seg3, seg3, lpt, gates_t, ys, lat, mods)


def _block_diag(blocks):
    H, a, b = blocks.shape
    eye = jnp.eye(H, dtype=blocks.dtype)
    return (eye[:, None, :, None] * blocks[:, :, None, :]).reshape(H * a, H * b)


def _head_slabs(w, width):
    lead = w.shape[:-1]
    w = w.reshape(lead + (MLA_HEADS, width))
    w = jnp.pad(w, [(0, 0)] * len(lead) + [(0, 0), (0, HEAD_PAD - width)])
    return w.reshape(lead + (MLA_HEADS * HEAD_PAD,))


def _rope_tables(rows, n_ctx):
    n_freq = MLA_ROPE // 4
    inv = ROPE_BASE ** (-jnp.arange(n_freq, dtype=F32) / n_freq)
    r = jnp.repeat(jnp.arange(rows, dtype=F32), GRID_W)
    col = jnp.tile(jnp.arange(GRID_W, dtype=F32), rows)
    ang = jnp.concatenate([r[:, None] * inv, col[:, None] * inv], axis=-1)
    cos, sin = jnp.cos(ang), jnp.sin(ang)
    n = cos.shape[0]
    half = MLA_ROPE // 2
    pad = HEAD_PAD - MLA_QK
    c_tab = jnp.concatenate([jnp.ones((n, MLA_NOPE), F32), cos, cos, jnp.ones((n, pad), F32)], axis=1)
    s_tab = jnp.concatenate([jnp.zeros((n, MLA_NOPE), F32), -sin, sin, jnp.zeros((n, pad), F32)], axis=1)
    c_tab = jnp.concatenate([c_tab, jnp.ones((n_ctx, HEAD_PAD), F32)], axis=0)
    s_tab = jnp.concatenate([s_tab, jnp.zeros((n_ctx, HEAD_PAD), F32)], axis=0)
    del half
    return c_tab, s_tab


def kernel(x, c, ctx, c_ctx, w_mod, b_mod, norm_mix, norm_ffn, w_in, w_out, pool_w, pool_scale, mla_q_a_norm, mla_w_uq, mla_kv_a_norm, mla_w_ukv, mla_q_norm, mla_k_norm, sgu_norm_g, sgu_norm_b, sgu_ws, sgu_b, lru_conv_w, lru_conv_b, lru_wa, lru_ba, lru_wx, lru_bx, lru_lambda, router_w, router_b, moe_w_gu, moe_b_gu, moe_w_down, moe_b_down):
    B, N, D = x.shape
    n_ctx = ctx.shape[1]
    L = w_mod.shape[0]
    T = N + n_ctx
    assert B == 1 and N % TILE == 0 and n_ctx == TILE and T % TOK_TILE == 0 and TOK_TILE % SGU_CHUNK == 0
    nt = T // TILE
    nt_lat = N // TILE

    lat = jnp.concatenate([x[0], ctx[0]], axis=0)
    crows = jnp.zeros((8, D), F32).at[0].set(c[0]).at[1].set(c_ctx)
    mods_all = _mod_call(crows, w_mod, b_mod)
    c_tab, s_tab = _rope_tables(N // GRID_W, n_ctx)
    tri = jnp.triu(jnp.ones((TILE, TILE), F32), k=1).astype(BF16)
    ltri = jnp.tril(jnp.ones((N_EXPERTS, N_EXPERTS), F32), k=-1).astype(BF16)

    block_u = MOE_BM // SEG
    n_blocks = -(-(T * TOP_K + nt * N_EXPERTS * (SEG - 1) + N_EXPERTS * (MOE_BM - 1)) // MOE_BM)
    n_rows = n_blocks * MOE_BM

    for l in range(L):
        mods = mods_all[l, :2].reshape(2, 6, D)

        wa_, wqa, wkva, wkr, wc_, wdx, wdg = jnp.split(
            w_in[l], [256, 448, 576, 608, 1120, 1376], axis=1)
        wkr_placed = jnp.pad(wkr.reshape(D, 1, MLA_ROPE),
                             ((0, 0), (0, 0), (MLA_NOPE, HEAD_PAD - MLA_QK)))
        wkr_placed = jnp.tile(wkr_placed, (1, MLA_HEADS, 1)).reshape(D, MLA_HEADS * HEAD_PAD)
        win = jnp.concatenate(
            [wa_, jnp.pad(wqa, ((0, 0), (0, 256 - MLA_Q_RANK))), wkva, wkr_placed, wc_, wdx, wdg],
            axis=1).astype(BF16)
        assert win.shape[1] == IN_COLS_P
        gqa = jnp.pad(mla_q_a_norm[l], (0, 256 - MLA_Q_RANK)).reshape(1, 256)
        wuq = jnp.pad(_head_slabs(mla_w_uq[l], MLA_QK), ((0, 256 - MLA_Q_RANK), (0, 0))).astype(BF16)
        gkva = mla_kv_a_norm[l].reshape(1, MLA_KV_RANK)
        wukv3 = mla_w_ukv[l].reshape(MLA_KV_RANK, MLA_HEADS, MLA_NOPE + MLA_V)
        wuk = _head_slabs(wukv3[:, :, :MLA_NOPE].reshape(MLA_KV_RANK, -1), MLA_NOPE)
        wuv = _head_slabs(wukv3[:, :, MLA_NOPE:].reshape(MLA_KV_RANK, -1), MLA_V)
        wukv = jnp.concatenate([wuk, wuv], axis=1).astype(BF16)
        gq = jnp.pad(mla_q_norm[l], (0, HEAD_PAD - MLA_QK)).reshape(1, HEAD_PAD)
        gk = jnp.pad(mla_k_norm[l], (0, HEAD_PAD - MLA_QK)).reshape(1, HEAD_PAD)
        wscat = sgu_ws[l].transpose(1, 0, 2).reshape(SGU_CHUNK, SGU_HEADS * SGU_CHUNK).astype(BF16)
        sbias = jnp.repeat(sgu_b[l].T, SGU_HD, axis=1)

        pa, q, k, vt, c_s, xd, gd = _inproj_call(
            lat, mods, norm_mix[l].reshape(1, D), win, gqa, wuq, gkva, wukv, gq, gk, c_tab, s_tab,
            sgu_norm_g[l].reshape(1, GROUP_W), sgu_norm_b[l].reshape(1, GROUP_W), wscat, sbias, N)

        wpool = _block_diag(pool_w[l]).astype(BF16)
        wlru = jnp.concatenate(
            [_block_diag(lru_wa[l, 0]), _block_diag(lru_wx[l, 0]),
             _block_diag(lru_wa[l, 1]), _block_diag(lru_wx[l, 1])], axis=1).astype(BF16)
        blru = jnp.concatenate([lru_ba[l, 0], lru_bx[l, 0], lru_ba[l, 1], lru_bx[l, 1]]).reshape(1, -1)
        a_s, hf, ab, bb = _seq_fwd_call(
            pa, xd, wpool, pool_scale[l].reshape(1, GROUP_W), lru_conv_w[l],
            lru_conv_b[l].reshape(1, GROUP_W), wlru, blru, lru_lambda[l], nt_lat, N, n_ctx)
        d_s = _seq_bwd_call(ab, bb, hf, gd)

        tk = next(t for t in (3328, 1280, TILE) if T % t == 0)
        b_lat = _attn_call(q, k, vt, N, 512, 0, T, tk, 0)
        if l < L - 1:
            b_ctx = _attn_call(q, k, vt, n_ctx, n_ctx, N // n_ctx, n_ctx, n_ctx, N // n_ctx)
        else:
            b_ctx = jnp.zeros((n_ctx, MLA_HEADS * MLA_V), BF16)
        b_s = jnp.concatenate([b_lat, b_ctx], axis=0)

        rw_hi = router_w[l].astype(BF16)
        rw_lo = (router_w[l] - rw_hi.astype(F32)).astype(BF16)
        lane_pad = ((0, 0), (0, LANE - N_EXPERTS))
        rw = jnp.concatenate([jnp.pad(rw_hi, lane_pad), jnp.pad(rw_lo, lane_pad)], axis=1)
        lat2, fin, top_e, gates_t = _outproj_call(
            a_s, b_s, c_s, d_s, lat, mods, w_out[l].astype(BF16), norm_ffn[l].reshape(1, D),
            rw, router_b[l].reshape(N_EXPERTS, 1), N)
        lp, lpt, seg, tot = _plan_call(top_e, tri, ltri)

        seg = seg.reshape(nt, N_EXPERTS, LANE)
        tot_u = tot[:, 0].astype(jnp.int32)
        padded_u = (tot_u + block_u - 1) // block_u * block_u
        pad_end_u = jnp.cumsum(padded_u)
        pad_start_u = pad_end_u - padded_u
        seg3 = jnp.concatenate(
            [seg[:, :, 0], seg[:, :, 1], seg[:, :, 2] + pad_start_u[None, :],
             jnp.zeros((nt, LANE - 3 * N_EXPERTS), jnp.int32)], axis=1).reshape(nt, 1, LANE)
        block_u0 = jnp.arange(n_blocks, dtype=jnp.int32) * block_u
        block_e = jnp.minimum(
            jnp.sum((pad_end_u[None, :] <= block_u0[:, None]).astype(jnp.int32), axis=1),
            N_EXPERTS - 1)
        n_used = (pad_end_u[-1:] // block_u).astype(jnp.int32)

        xs = _dispatch_call(pad_start_u + tot_u, pad_end_u, seg3, lp, fin, n_rows)
        ys = _expert_call(block_e, n_used, xs, moe_w_gu, moe_b_gu, moe_w_down, moe_b_down, l)
        lat = _combine_call(seg3, lpt, ys, gates_t, lat2, mods, nt_lat, T)

    return lat[:N].reshape(B, N, D)
```

```python
import functools

import jax
import jax.numpy as jnp
from jax import lax
from jax.experimental import pallas as pl
from jax.experimental.pallas import tpu as pltpu

F32 = jnp.float32
BF16 = jnp.bfloat16
HIGHEST = lax.Precision.HIGHEST

EPS = 1e-6
LOG2_E = 1.4426950408889634
GRID_W = 64
GROUP_W = 256
POOL_WINDOWS = (2, 4, 8, 16)
POOL_CH = GROUP_W // len(POOL_WINDOWS)
MLA_HEADS = 4
MLA_NOPE = 64
MLA_ROPE = 32
MLA_QK = MLA_NOPE + MLA_ROPE
MLA_V = 64
MLA_Q_RANK = 192
MLA_KV_RANK = 128
ROPE_BASE = 10000.0
SGU_HEADS = 4
SGU_HD = GROUP_W // SGU_HEADS
SGU_CHUNK = 128
LRU_HEADS = 4
LRU_HD = GROUP_W // LRU_HEADS
CONV_W = 4
CONV_LEFT = 1
LRU_C = 8.0
N_EXPERTS = 32
TOP_K = 4
SWIGLU_LIMIT = 7.0
SWIGLU_ALPHA = 1.702

LANE = 128
HEAD_PAD = LANE
TILE = 256
TOK_TILE = 640
HALO = 16
V_ROWS = 80
ATTN_CHUNK = 256
MOE_BM = 512
VMEM_LIMIT = 56 * 1024 * 1024

C_A = 0
C_QA = C_A + GROUP_W
C_KVA = C_QA + 256
C_KR = C_KVA + MLA_KV_RANK
C_CU = C_KR + MLA_HEADS * HEAD_PAD
C_CV = C_CU + GROUP_W
C_DX = C_CV + GROUP_W
C_DG = C_DX + GROUP_W
IN_COLS_P = C_DG + GROUP_W


def _params(sem, vmem=VMEM_LIMIT):
    return pltpu.CompilerParams(dimension_semantics=sem, vmem_limit_bytes=vmem)


def _rms(x, n):
    return x * lax.rsqrt(jnp.sum(x * x, axis=-1, keepdims=True) * (1.0 / n) + EPS)


def _mod_kernel(c_ref, w_ref, b_ref, o_ref):
    cv = c_ref[...]
    s = cv * jax.nn.sigmoid(cv)
    o_ref[0] = jnp.dot(s, w_ref[0], precision=HIGHEST, preferred_element_type=F32) + b_ref[0]


def _mod_call(crows, w_mod, b_mod):
    L, D, M = w_mod.shape
    bn = 1536
    return pl.pallas_call(
        _mod_kernel,
        grid=(L, M // bn),
        in_specs=[
            pl.BlockSpec((8, D), lambda l, j: (0, 0)),
            pl.BlockSpec((1, D, bn), lambda l, j: (l, 0, j)),
            pl.BlockSpec((1, 1, bn), lambda l, j: (l, 0, j)),
        ],
        out_specs=pl.BlockSpec((1, 8, bn), lambda l, j: (l, 0, j)),
        out_shape=jax.ShapeDtypeStruct((L, 8, M), F32),
        compiler_params=_params(("arbitrary", "arbitrary")),
        name="mod",
    )(crows, w_mod, b_mod.reshape(L, 1, M))


def _rope_heads(x, gain, c_tab, s_tab, lane, scale):
    outs = []
    for h in range(MLA_HEADS):
        xh = x[:, h * HEAD_PAD:(h + 1) * HEAD_PAD]
        xn = _rms(xh, MLA_QK) * gain
        swap = jnp.where(lane < MLA_NOPE + MLA_ROPE // 2,
                         pltpu.roll(xn, HEAD_PAD - MLA_ROPE // 2, axis=1),
                         pltpu.roll(xn, MLA_ROPE // 2, axis=1))
        r = xn * c_tab + swap * s_tab
        if scale != 1.0:
            r = r * scale
        outs.append(r.astype(BF16))
    return jnp.concatenate(outs, axis=1)


def _mod_row(m_ref, is_ctx, k):
    return jnp.where(is_ctx, m_ref[1, k:k + 1, :], m_ref[0, k:k + 1, :])


def _is_ctx_rows(tm, n_lat):
    row = pl.program_id(0) * tm + lax.broadcasted_iota(jnp.int32, (tm, 1), 0)
    return row >= n_lat


def _inproj_kernel(n_lat, x_ref, m_ref, gn_ref, win_ref, gqa_ref, wuq_ref, gkva_ref, wukv_ref,
                   gq_ref, gk_ref, ct_ref, st_ref, sg_ref, sb_ref, ws_ref, sbias_ref,
                   pa_ref, q_ref, k_ref, vt_ref, c_ref, xd_ref, gd_ref):
    x = x_ref[...]
    is_ctx = _is_ctx_rows(x.shape[0], n_lat)
    h = _rms(x, x.shape[-1]) * gn_ref[...]
    h = h * (1.0 + _mod_row(m_ref, is_ctx, 1)) + _mod_row(m_ref, is_ctx, 0)
    p = jnp.dot(h.astype(BF16), win_ref[...], preferred_element_type=F32)

    pa_ref[...] = p[:, C_A:C_A + GROUP_W]
    xd_ref[...] = p[:, C_DX:C_DX + GROUP_W]
    gd_ref[...] = jax.nn.gelu(p[:, C_DG:C_DG + GROUP_W])

    lane = lax.broadcasted_iota(jnp.int32, (x.shape[0], HEAD_PAD), 1)
    c_tab = ct_ref[...]
    s_tab = st_ref[...]
    qa = _rms(p[:, C_QA:C_QA + 256], MLA_Q_RANK) * gqa_ref[...]
    q = jnp.dot(qa.astype(BF16), wuq_ref[...], preferred_element_type=F32)
    q_ref[...] = _rope_heads(q, gq_ref[...], c_tab, s_tab, lane, MLA_QK ** -0.5 * LOG2_E)

    kva = _rms(p[:, C_KVA:C_KVA + MLA_KV_RANK], MLA_KV_RANK) * gkva_ref[...]
    kv = jnp.dot(kva.astype(BF16), wukv_ref[...], preferred_element_type=F32)
    kpre = kv[:, :MLA_HEADS * HEAD_PAD] + p[:, C_KR:C_KR + MLA_HEADS * HEAD_PAD]
    k_ref[...] = _rope_heads(kpre, gk_ref[...], c_tab, s_tab, lane, 1.0)
    v = kv[:, MLA_HEADS * HEAD_PAD:]
    slab_lane = lax.broadcasted_iota(jnp.int32, v.shape, 1) % HEAD_PAD
    vt_ref[...] = jnp.where(slab_lane == MLA_V, 1.0, v).T.astype(BF16)

    z = jax.nn.gelu(p[:, C_CU:C_CU + 2 * GROUP_W])
    u = z[:, :GROUP_W]
    v = z[:, GROUP_W:]
    mu = jnp.mean(v, axis=-1, keepdims=True)
    vc = v - mu
    var = jnp.mean(vc * vc, axis=-1, keepdims=True)
    vn = (vc * lax.rsqrt(var + EPS) * sg_ref[...] + sb_ref[...]).astype(BF16)
    head_of_lane = lax.broadcasted_iota(jnp.int32, (SGU_CHUNK, GROUP_W), 1) // SGU_HD
    zero = jnp.zeros((SGU_CHUNK, GROUP_W), BF16)
    for cch in range(x.shape[0] // SGU_CHUNK):
        rows = slice(cch * SGU_CHUNK, (cch + 1) * SGU_CHUNK)
        vch = vn[rows]
        stacked = jnp.concatenate(
            [jnp.where(head_of_lane == hh, vch, zero) for hh in range(SGU_HEADS)], axis=0)
        mixed = jnp.dot(ws_ref[...], stacked, preferred_element_type=F32) + sbias_ref[...]
        c_ref[rows, :] = (u[rows] * mixed).astype(BF16)


def _inproj_call(xcat, mods, gn, win, gqa, wuq, gkva, wukv, gq, gk, ctab, stab, sg, sb, wscat, sbias,
                 n_lat):
    T, D = xcat.shape
    tm = TILE
    nt = T // tm
    row = lambda i: (i, 0)

    def full(a):
        return pl.BlockSpec(a.shape, lambda i: (0,) * a.ndim)

    out_shapes = (
        jax.ShapeDtypeStruct((T, GROUP_W), F32),
        jax.ShapeDtypeStruct((T, MLA_HEADS * HEAD_PAD), BF16),
        jax.ShapeDtypeStruct((T, MLA_HEADS * HEAD_PAD), BF16),
        jax.ShapeDtypeStruct((MLA_HEADS * HEAD_PAD, T), BF16),
        jax.ShapeDtypeStruct((T, GROUP_W), BF16),
        jax.ShapeDtypeStruct((T, GROUP_W), F32),
        jax.ShapeDtypeStruct((T, GROUP_W), F32),
    )
    out_specs = [pl.BlockSpec((tm, s.shape[1]), row) for s in out_shapes]
    out_specs[3] = pl.BlockSpec((MLA_HEADS * HEAD_PAD, tm), lambda i: (0, i))
    return pl.pallas_call(
        functools.partial(_inproj_kernel, n_lat),
        grid=(nt,),
        in_specs=[
            pl.BlockSpec((tm, D), row),
            full(mods),
            full(gn), full(win), full(gqa), full(wuq), full(gkva), full(wukv), full(gq), full(gk),
            pl.BlockSpec((tm, HEAD_PAD), row),
            pl.BlockSpec((tm, HEAD_PAD), row),
            full(sg), full(sb), full(wscat), full(sbias),
        ],
        out_specs=tuple(out_specs),
        out_shape=out_shapes,
        compiler_params=_params(("parallel",)),
        name="inproj",
    )(xcat, mods, gn, win, gqa, wuq, gkva, wukv, gq, gk, ctab, stab, sg, sb, wscat, sbias)


def _shift_rows(x, d, fill, reverse):
    n = x.shape[0]
    t = lax.broadcasted_iota(jnp.int32, x.shape, 0)
    if reverse:
        return jnp.where(t < n - d, pltpu.roll(x, n - d, axis=0), fill)
    return jnp.where(t >= d, pltpu.roll(x, d, axis=0), fill)


def _tile_scan(a, b, reverse):
    d = 1
    while d < a.shape[0]:
        a_s = _shift_rows(a, d, 1.0, reverse)
        b_s = _shift_rows(b, d, 0.0, reverse)
        b = a * b_s + b
        a = a * a_s
        d *= 2
    return a, b


def _lru_coeffs(xc, proj, sp, d):
    r = jax.nn.sigmoid(proj[:, (2 * d) * GROUP_W:(2 * d + 1) * GROUP_W])
    i = jax.nn.sigmoid(proj[:, (2 * d + 1) * GROUP_W:(2 * d + 2) * GROUP_W])
    log_a = -LRU_C * r * sp[d:d + 1]
    a = jnp.exp(log_a)
    drive = jnp.sqrt(1.0 - jnp.exp(2.0 * log_a)) * (i * xc)
    return a, drive


def _seq_fwd_kernel(nt_lat, n_lat, n_ctx,
                    pa_ref, pa_prev_ref, pa_next_ref, xd_ref, xd_prev_ref, xd_next_ref,
                    wpool_ref, pscale_ref, cw_ref, cb_ref, wlru_ref, blru_ref, lam_ref,
                    a_out_ref, hf_ref, ab_ref, bb_ref,
                    ext_ref, carry_ref):
    j = pl.program_id(0)
    nt = pl.num_programs(0)
    ti = (j + nt_lat) % nt
    is_ctx = ti >= nt_lat
    seq_first = jnp.logical_or(ti == 0, ti == nt_lat)
    seq_last = jnp.logical_or(ti == nt_lat - 1, ti == nt - 1)
    t_loc = (ti - jnp.where(is_ctx, nt_lat, 0)) * TILE
    n_seq = jnp.where(is_ctx, n_ctx, n_lat)

    @pl.when(j == 0)
    def _():
        carry_ref[...] = jnp.zeros_like(carry_ref)

    def load_ext(cur_ref, prev_ref, next_ref):
        ext_ref[0:HALO, :] = jnp.where(seq_first, 0.0, prev_ref[...])
        ext_ref[HALO:HALO + TILE, :] = cur_ref[...]
        ext_ref[HALO + TILE:, :] = jnp.where(seq_last, 0.0, next_ref[...])

    def win(off):
        return ext_ref[pl.ds(HALO + off, TILE), :]

    load_ext(pa_ref, pa_prev_ref, pa_next_ref)
    x = pa_ref[...]
    t = t_loc + lax.broadcasted_iota(jnp.int32, (TILE, GROUP_W), 0)
    lane = lax.broadcasted_iota(jnp.int32, (TILE, GROUP_W), 1)
    ext = ext_ref[...]
    n_ext = ext.shape[0]
    sums = {2: ext + pltpu.roll(ext, 1, axis=0)}
    w = 2
    while w < max(POOL_WINDOWS):
        prev = sums[w]
        sums[2 * w] = pltpu.roll(prev, w // 2, axis=0) + pltpu.roll(prev, n_ext - w // 2, axis=0)
        w *= 2
    mean = jnp.zeros((TILE, GROUP_W), F32)
    for g, w in enumerate(POOL_WINDOWS):
        acc = sums[w][HALO:HALO + TILE]
        cnt = jnp.minimum(t + w // 2, n_seq) - jnp.maximum(t - w // 2, 0)
        mean = jnp.where(lane // POOL_CH == g, acc / cnt.astype(F32), mean)
    diff = (mean - x).astype(BF16)
    pooled = jnp.dot(diff, wpool_ref[...], preferred_element_type=F32) * pscale_ref[...]
    a_out_ref[...] = pooled.astype(BF16)

    load_ext(xd_ref, xd_prev_ref, xd_next_ref)
    xc = jnp.zeros((TILE, GROUP_W), F32) + cb_ref[...]
    for k in range(CONV_W):
        xc = xc + win(k - CONV_LEFT) * cw_ref[k:k + 1, :]

    proj = jnp.dot(xc.astype(BF16), wlru_ref[...], preferred_element_type=F32) + blru_ref[...]
    lam = lam_ref[...]
    sp = jnp.maximum(-lam, 0.0) + jnp.log(1.0 + jnp.exp(-jnp.abs(lam)))
    a_f, b_f = _lru_coeffs(xc, proj, sp, 0)
    a_b, b_b = _lru_coeffs(xc, proj, sp, 1)
    ab_ref[...] = a_b
    bb_ref[...] = b_b

    big_a, big_b = _tile_scan(a_f, b_f, reverse=False)
    hf = big_b + big_a * carry_ref[0:1, :]
    hf_ref[...] = hf
    carry_ref[0:1, :] = hf[TILE - 1:TILE, :]


def _seq_fwd_call(pa, xd, wpool, pscale, cw, cb, wlru, blru, lam, nt_lat, n_lat, n_ctx):
    T = pa.shape[0]
    nt = T // TILE
    hb = TILE // HALO
    n_halo = T // HALO

    def tile_of(j):
        return (j + nt_lat) % nt

    cur = lambda j: (tile_of(j), 0)
    prev = lambda j: (jnp.maximum(tile_of(j) * hb - 1, 0), 0)
    nxt = lambda j: (jnp.minimum((tile_of(j) + 1) * hb, n_halo - 1), 0)
    fixed = lambda j: (0, 0)

    def full(a):
        return pl.BlockSpec(a.shape, fixed)

    tile_spec = pl.BlockSpec((TILE, GROUP_W), cur)
    out_shapes = (
        jax.ShapeDtypeStruct((T, GROUP_W), BF16),
        jax.ShapeDtypeStruct((T, GROUP_W), F32),
        jax.ShapeDtypeStruct((T, GROUP_W), F32),
        jax.ShapeDtypeStruct((T, GROUP_W), F32),
    )
    return pl.pallas_call(
        functools.partial(_seq_fwd_kernel, nt_lat, n_lat, n_ctx),
        grid=(nt,),
        in_specs=[
            tile_spec, pl.BlockSpec((HALO, GROUP_W), prev), pl.BlockSpec((HALO, GROUP_W), nxt),
            tile_spec, pl.BlockSpec((HALO, GROUP_W), prev), pl.BlockSpec((HALO, GROUP_W), nxt),
            full(wpool), full(pscale), full(cw), full(cb), full(wlru), full(blru), full(lam),
        ],
        out_specs=tuple(pl.BlockSpec((TILE, GROUP_W), cur) for _ in out_shapes),
        out_shape=out_shapes,
        scratch_shapes=[pltpu.VMEM((TILE + 2 * HALO, GROUP_W), F32), pltpu.VMEM((8, GROUP_W), F32)],
        compiler_params=_params(("arbitrary",)),
        name="seq_fwd",
    )(pa, pa, pa, xd, xd, xd, wpool, pscale, cw, cb, wlru, blru, lam)


def _seq_bwd_kernel(ab_ref, bb_ref, hf_ref, gd_ref, d_out_ref, carry_ref):
    @pl.when(pl.program_id(0) == 0)
    def _():
        carry_ref[...] = jnp.zeros_like(carry_ref)

    big_a, big_b = _tile_scan(ab_ref[...], bb_ref[...], reverse=True)
    hb = big_b + big_a * carry_ref[0:1, :]
    carry_ref[0:1, :] = hb[0:1, :]
    d_out_ref[...] = (gd_ref[...] * (hf_ref[...] + hb)).astype(BF16)


def _seq_bwd_call(ab, bb, hf, gd):
    T = ab.shape[0]
    nt = T // TILE
    spec = pl.BlockSpec((TILE, GROUP_W), lambda j: (nt - 1 - j, 0))
    return pl.pallas_call(
        _seq_bwd_kernel,
        grid=(nt,),
        in_specs=[spec, spec, spec, spec],
        out_specs=spec,
        out_shape=jax.ShapeDtypeStruct((T, GROUP_W), BF16),
        scratch_shapes=[pltpu.VMEM((8, GROUP_W), F32)],
        compiler_params=_params(("arbitrary",)),
        name="seq_bwd",
    )(ab, bb, hf, gd)


def _attn_kernel(q_ref, k_ref, vt_ref, o_ref, m_ref, acc_ref):
    kv = pl.program_id(1)

    @pl.when(kv == 0)
    def _():
        m_ref[...] = jnp.full_like(m_ref, -jnp.inf)
        acc_ref[...] = jnp.zeros_like(acc_ref)

    def head_cols(h):
        return slice(h * HEAD_PAD, (h + 1) * HEAD_PAD)

    ck = ATTN_CHUNK if k_ref.shape[0] % ATTN_CHUNK == 0 else k_ref.shape[0]
    items = [(c, h) for c in range(k_ref.shape[0] // ck) for h in range(MLA_HEADS)]

    def scores(c, h):
        cols = head_cols(h)
        return lax.dot_general(k_ref[c * ck:(c + 1) * ck, cols], q_ref[:, cols], (((1,), (1,)), ((), ())),
                               preferred_element_type=F32)

    def probs(h, s):
        m_prev = m_ref[h]
        m_new = jnp.maximum(m_prev, jnp.max(s, axis=0, keepdims=True))
        m_ref[h] = m_new
        return jnp.exp2(m_prev - m_new), jnp.exp2(s - m_new).astype(BF16)

    def accumulate(c, h, alpha, p):
        rows = slice(h * HEAD_PAD, h * HEAD_PAD + V_ROWS)
        acc_ref[h, :V_ROWS] = alpha * acc_ref[h, :V_ROWS] + jnp.dot(vt_ref[rows, c * ck:(c + 1) * ck], p,
                                                                    preferred_element_type=F32)

    s_next = scores(*items[0])
    pending = None
    for n, (c, h) in enumerate(items):
        s_cur = s_next
        if n + 1 < len(items):
            s_next = scores(*items[n + 1])
        if pending is not None:
            accumulate(*pending)
        pending = (c, h) + probs(h, s_cur)
    accumulate(*pending)

    @pl.when(kv == pl.num_programs(1) - 1)
    def _():
        outs = []
        for h in range(MLA_HEADS):
            a = acc_ref[h]
            outs.append((a / a[MLA_V:MLA_V + 1, :]).T[:, :MLA_V])
        o_ref[...] = jnp.concatenate(outs, axis=1).astype(BF16)


def _attn_call(q, k, vt, n_q, tq, q_blk0, n_k, tk, k_blk0):
    width = MLA_HEADS * HEAD_PAD
    return pl.pallas_call(
        _attn_kernel,
        grid=(n_q // tq, n_k // tk),
        in_specs=[
            pl.BlockSpec((tq, width), lambda i, j: (q_blk0 + i, 0)),
            pl.BlockSpec((tk, width), lambda i, j: (k_blk0 + j, 0)),
            pl.BlockSpec((width, tk), lambda i, j: (0, k_blk0 + j)),
        ],
        out_specs=pl.BlockSpec((tq, MLA_HEADS * MLA_V), lambda i, j: (i, 0)),
        out_shape=jax.ShapeDtypeStruct((n_q, MLA_HEADS * MLA_V), BF16),
        scratch_shapes=[
            pltpu.VMEM((MLA_HEADS, 1, tq), F32),
            pltpu.VMEM((MLA_HEADS, HEAD_PAD, tq), F32),
        ],
        compiler_params=_params(("parallel", "arbitrary")),
        name="attn",
    )(q, k, vt)


def _outproj_kernel(n_lat, a_ref, b_ref, c_ref, d_ref, x_ref, m_ref, wout_ref, gn_ref, rw_ref, rb_ref,
                    lat_ref, fin_ref, tope_ref, gatet_ref):
    tm = x_ref.shape[0]
    n_e = rb_ref.shape[0]
    is_ctx = _is_ctx_rows(tm, n_lat)
    y = jnp.dot(jnp.concatenate([a_ref[...], b_ref[...], c_ref[...], d_ref[...]], axis=1),
                wout_ref[...], preferred_element_type=F32)
    lat = x_ref[...] + _mod_row(m_ref, is_ctx, 2) * y
    lat_ref[...] = lat
    f = _rms(lat, lat.shape[-1]) * gn_ref[...]
    f = f * (1.0 + _mod_row(m_ref, is_ctx, 4)) + _mod_row(m_ref, is_ctx, 3)

    f_hi = f.astype(BF16)
    fin_ref[...] = f_hi
    f_lo = (f - f_hi.astype(F32)).astype(BF16)
    hi_prod = jnp.dot(f_hi, rw_ref[...], preferred_element_type=F32)
    lo_prod = jnp.dot(f_lo, rw_ref[:, :LANE], preferred_element_type=F32)
    logits_rows = hi_prod[:, :LANE] + hi_prod[:, LANE:] + lo_prod
    logits = logits_rows.T[:n_e] + rb_ref[...]
    e_iota = lax.broadcasted_iota(jnp.int32, (n_e, tm), 0)
    vals, idxs = [], []
    for _ in range(TOP_K):
        mx = jnp.max(logits, axis=0, keepdims=True)
        idx = jnp.min(jnp.where(logits == mx, e_iota, n_e), axis=0, keepdims=True)
        vals.append(mx)
        idxs.append(idx)
        logits = jnp.where(e_iota == idx, -jnp.inf, logits)
    ex = [jnp.exp(vv - vals[0]) for vv in vals]
    den = ex[0] + ex[1] + ex[2] + ex[3]
    row8 = lax.broadcasted_iota(jnp.int32, (8, tm), 0)
    row128 = lax.broadcasted_iota(jnp.int32, (LANE, tm), 0)
    tope = jnp.zeros((8, tm), jnp.int32)
    gates = jnp.zeros((LANE, tm), F32)
    for k in range(TOP_K):
        tope = jnp.where(row8 == k, idxs[k], tope)
        gates = jnp.where(row128 == k, ex[k] / den, gates)
    tope_ref[...] = tope
    gatet_ref[...] = gates.T


def _outproj_call(a, b, c, d, xcat, mods, wout, gn, rw, rb, n_lat):
    T, D = xcat.shape
    tm = TOK_TILE
    row = lambda i: (i, 0)

    def full(arr):
        return pl.BlockSpec(arr.shape, lambda i: (0,) * arr.ndim)

    g_spec = pl.BlockSpec((tm, GROUP_W), row)
    out_shapes = (
        jax.ShapeDtypeStruct((T, D), F32),
        jax.ShapeDtypeStruct((T, D), BF16),
        jax.ShapeDtypeStruct((8, T), jnp.int32),
        jax.ShapeDtypeStruct((T, LANE), F32),
    )
    out_specs = (
        pl.BlockSpec((tm, D), row), pl.BlockSpec((tm, D), row),
        pl.BlockSpec((8, tm), lambda i: (0, i)),
        pl.BlockSpec((tm, LANE), row),
    )
    return pl.pallas_call(
        functools.partial(_outproj_kernel, n_lat),
        grid=(T // tm,),
        in_specs=[
            g_spec, g_spec, g_spec, g_spec,
            pl.BlockSpec((tm, D), row),
            full(mods), full(wout), full(gn), full(rw), full(rb),
        ],
        out_specs=out_specs,
        out_shape=out_shapes,
        compiler_params=_params(("parallel",)),
        name="outproj",
    )(a, b, c, d, xcat, mods, wout, gn, rw, rb)


SEG = 8
BIG_UNITS = 4
BIG_ROWS = BIG_UNITS * SEG
SORT_ROWS = TOP_K * TILE + N_EXPERTS * SEG
SEG_START, SEG_COUNT, SEG_DEST = 0, N_EXPERTS, 2 * N_EXPERTS


def _plan_kernel(tope_ref, tri_ref, ltri_ref, lp_ref, lpt_ref, seg_ref, tot_ref, carry_ref):
    @pl.when(pl.program_id(0) == 0)
    def _():
        carry_ref[...] = jnp.zeros_like(carry_ref)

    tope = tope_ref[...]
    n_e = ltri_ref.shape[0]
    tm = tope.shape[1]
    e_iota = lax.broadcasted_iota(jnp.int32, (n_e, tm), 0)
    onehot = jnp.concatenate(
        [jnp.where(e_iota == tope[k:k + 1, :], 1.0, 0.0) for k in range(TOP_K)], axis=0)
    prefix = jnp.dot(onehot.astype(BF16), tri_ref[...], preferred_element_type=F32)
    cnt_k = [jnp.sum(onehot[k * n_e:(k + 1) * n_e], axis=1, keepdims=True) for k in range(TOP_K)]
    cnt = cnt_k[0] + cnt_k[1] + cnt_k[2] + cnt_k[3]
    units = jnp.floor((cnt + (SEG - 1)) * (1.0 / SEG))
    units_b = jnp.broadcast_to(units, (n_e, LANE))
    start_u = jnp.dot(ltri_ref[...], units_b.astype(BF16), preferred_element_type=F32)
    base = start_u[:, 0:1] * SEG
    row8 = lax.broadcasted_iota(jnp.int32, (8, tm), 0)
    row128 = lax.broadcasted_iota(jnp.int32, (LANE, tm), 0)
    lp8 = jnp.zeros((8, tm), F32)
    lp128 = jnp.zeros((LANE, tm), F32)
    for k in range(TOP_K):
        hot = onehot[k * n_e:(k + 1) * n_e]
        lp_k = jnp.sum(hot * (prefix[k * n_e:(k + 1) * n_e] + base), axis=0, keepdims=True)
        lp8 = jnp.where(row8 == k, lp_k, lp8)
        lp128 = jnp.where(row128 == k, lp_k, lp128)
        base = base + cnt_k[k]
    lp_ref[...] = lp8.astype(jnp.int32)
    lpt_ref[...] = lp128.T.astype(jnp.int32)
    lane = lax.broadcasted_iota(jnp.int32, (n_e, LANE), 1)
    carry = carry_ref[...]
    seg = jnp.where(lane == 0, start_u, jnp.where(lane == 1, units_b, jnp.where(lane == 2, carry, 0.0)))
    seg_ref[...] = seg.astype(jnp.int32)
    carry_ref[...] = carry + units_b
    tot_ref[...] = carry + units_b


def _plan_call(top_e, tri, ltri):
    T = top_e.shape[1]
    nt = T // TILE
    n_e = ltri.shape[0]
    fixed = lambda i: (0, 0)
    out_shapes = (
        jax.ShapeDtypeStruct((8, T), jnp.int32),
        jax.ShapeDtypeStruct((T, LANE), jnp.int32),
        jax.ShapeDtypeStruct((nt * n_e, LANE), jnp.int32),
        jax.ShapeDtypeStruct((n_e, LANE), F32),
    )
    return pl.pallas_call(
        _plan_kernel,
        grid=(nt,),
        in_specs=[pl.BlockSpec((8, TILE), lambda i: (0, i)), pl.BlockSpec(tri.shape, fixed),
                  pl.BlockSpec(ltri.shape, fixed)],
        out_specs=(pl.BlockSpec((8, TILE), lambda i: (0, i)), pl.BlockSpec((TILE, LANE), lambda i: (i, 0)),
                   pl.BlockSpec((n_e, LANE), lambda i: (i, 0)), pl.BlockSpec((n_e, LANE), fixed)),
        out_shape=out_shapes,
        scratch_shapes=[pltpu.VMEM((n_e, LANE), F32)],
        compiler_params=_params(("arbitrary",)),
        name="plan",
    )(top_e, tri, ltri)


def _chunk_copy(src, s_row, dst, d_row, sem, rows=SEG):
    return pltpu.make_async_copy(src.at[pl.ds(pl.multiple_of(s_row, SEG), rows)],
                                 dst.at[pl.ds(pl.multiple_of(d_row, SEG), rows)], sem)


def _chunk_counts(seg_ref, e):
    n = seg_ref[0, 0, SEG_COUNT + e]
    n_big = n // BIG_UNITS
    return n_big, n - n_big * BIG_UNITS


def _for_each_chunk(seg_ref, fn):
    def per_expert(e, totals):
        n_big, n_small = _chunk_counts(seg_ref, e)
        local = seg_ref[0, 0, SEG_START + e] * SEG
        dest = seg_ref[0, 0, SEG_DEST + e] * SEG

        def big(j, carry):
            fn(local + j * BIG_ROWS, dest + j * BIG_ROWS, BIG_ROWS)
            return carry

        def small(j, carry):
            off = n_big * BIG_ROWS + j * SEG
            fn(local + off, dest + off, SEG)
            return carry

        lax.fori_loop(0, n_big, big, 0)
        lax.fori_loop(0, n_small, small, 0)
        return totals[0] + n_big, totals[1] + n_small

    return lax.fori_loop(0, N_EXPERTS, per_expert, (0, 0))


def _wait_chunks(src, dst, sem, n_big, n_small):
    def wait_big(j, c):
        _chunk_copy(src, 0, dst, 0, sem, BIG_ROWS).wait()
        return c

    def wait_small(j, c):
        _chunk_copy(src, 0, dst, 0, sem).wait()
        return c

    lax.fori_loop(0, n_big, wait_big, 0)
    lax.fori_loop(0, n_small, wait_small, 0)


def _dispatch_kernel(lo_ref, hi_ref, seg_ref, lp_ref, fin_ref, xs_ref, sorted_ref, zero_ref, cnt_ref, sems, zero_sem,
                     free_sem):
    i = pl.program_id(0)
    slot = i % 2

    @pl.when(i == 0)
    def _():
        for n in range(4):
            cnt_ref[n] = 0
        zero_ref[...] = jnp.zeros_like(zero_ref)

        def per_expert(e, carry):
            def start(u, c):
                _chunk_copy(zero_ref, 0, xs_ref, u * SEG, zero_sem).start()
                return c

            def wait(u, c):
                _chunk_copy(zero_ref, 0, xs_ref, 0, zero_sem).wait()
                return c

            lax.fori_loop(lo_ref[e], hi_ref[e], start, 0)
            lax.fori_loop(lo_ref[e], hi_ref[e], wait, 0)
            return carry

        lax.fori_loop(0, N_EXPERTS, per_expert, 0)

    first_free = hi_ref[N_EXPERTS - 1] // (MOE_BM // SEG)
    n_blocks = xs_ref.shape[0] // MOE_BM

    def free_block_copy(b):
        return pltpu.make_async_copy(zero_ref, xs_ref.at[pl.ds(pl.multiple_of(b * MOE_BM, MOE_BM), MOE_BM)], free_sem)

    @pl.when(i == 0)
    def _():
        def start(b, c):
            free_block_copy(b).start()
            return c

        lax.fori_loop(first_free, n_blocks, start, 0)

    def drain(s):
        _wait_chunks(sorted_ref.at[s], xs_ref, sems.at[s], cnt_ref[2 * s], cnt_ref[2 * s + 1])

    drain(slot)
    lp = lp_ref[...]
    j_iota = lax.broadcasted_iota(jnp.int32, (SORT_ROWS, lp.shape[1]), 0)
    perm = jnp.zeros(j_iota.shape, F32)
    for k in range(TOP_K):
        perm = jnp.where(j_iota == lp[k:k + 1, :], 1.0, perm)
    sorted_ref[slot] = jnp.dot(perm.astype(BF16), fin_ref[...], preferred_element_type=F32)

    def send(local, dest, rows):
        _chunk_copy(sorted_ref.at[slot], local, xs_ref, dest, sems.at[slot], rows).start()

    n_big, n_small = _for_each_chunk(seg_ref, send)
    cnt_ref[2 * slot] = n_big
    cnt_ref[2 * slot + 1] = n_small

    @pl.when(i == pl.num_programs(0) - 1)
    def _():
        drain(slot)
        drain(1 - slot)

        def wait(b, c):
            free_block_copy(b).wait()
            return c

        lax.fori_loop(first_free, n_blocks, wait, 0)


def _dispatch_call(pad_lo, pad_hi, seg3, lp, fin, n_rows):
    nt = seg3.shape[0]
    D = fin.shape[1]
    grid_spec = pltpu.PrefetchScalarGridSpec(
        num_scalar_prefetch=2,
        grid=(nt,),
        in_specs=[
            pl.BlockSpec((1, 1, LANE), lambda i, lo, hi: (i, 0, 0), memory_space=pltpu.SMEM),
            pl.BlockSpec((8, TILE), lambda i, lo, hi: (0, i)),
            pl.BlockSpec((TILE, D), lambda i, lo, hi: (i, 0)),
        ],
        out_specs=pl.BlockSpec(memory_space=pl.ANY),
        scratch_shapes=[pltpu.VMEM((2, SORT_ROWS, D), F32), pltpu.VMEM((MOE_BM, D), F32), pltpu.SMEM((4,), jnp.int32),
                        pltpu.SemaphoreType.DMA((2,)), pltpu.SemaphoreType.DMA(()), pltpu.SemaphoreType.DMA(())],
    )
    return pl.pallas_call(
        _dispatch_kernel,
        grid_spec=grid_spec,
        out_shape=jax.ShapeDtypeStruct((n_rows, D), F32),
        compiler_params=_params(("arbitrary",)),
        name="dispatch",
    )(pad_lo, pad_hi, seg3, lp, fin)


def _expert_kernel(be_ref, nu_ref, xs_ref, wgu_ref, bgu_ref, wdn_ref, bdn_ref, ys_ref, wgu_bf, wdn_bf):
    i = pl.program_id(0)
    changed = jnp.logical_or(i == 0, be_ref[i] != be_ref[jnp.maximum(i - 1, 0)])

    @pl.when(changed)
    def _():
        wgu_bf[...] = wgu_ref[...].astype(BF16)
        wdn_bf[...] = wdn_ref[...].astype(BF16)

    @pl.when(i < nu_ref[0])
    def _():
        f = wdn_bf.shape[0]
        gu = jnp.dot(xs_ref[...].astype(BF16), wgu_bf[...], preferred_element_type=F32) + bgu_ref[...]
        g = jnp.minimum(gu[:, :f], SWIGLU_LIMIT)
        u = jnp.clip(gu[:, f:], -SWIGLU_LIMIT, SWIGLU_LIMIT)
        act = (u + 1.0) * (g * jax.nn.sigmoid(SWIGLU_ALPHA * g))
        ys_ref[...] = jnp.dot(act.astype(BF16), wdn_bf[...], preferred_element_type=F32) + bdn_ref[...]

    @pl.when(i >= nu_ref[0])
    def _():
        ys_ref[...] = jnp.zeros_like(ys_ref)


def _expert_call(block_e, n_used, xs, w_gu, b_gu, w_down, b_down, layer):
    n_rows, D = xs.shape
    _, E, _, F2 = w_gu.shape
    F = F2 // 2
    grid_spec = pltpu.PrefetchScalarGridSpec(
        num_scalar_prefetch=2,
        grid=(n_rows // MOE_BM,),
        in_specs=[
            pl.BlockSpec((MOE_BM, D), lambda i, be, nu: (jnp.minimum(i, nu[0] - 1), 0)),
            pl.BlockSpec((None, None, D, F2), lambda i, be, nu: (layer, be[i], 0, 0)),
            pl.BlockSpec((None, None, 1, F2), lambda i, be, nu: (layer, be[i], 0, 0)),
            pl.BlockSpec((None, None, F, D), lambda i, be, nu: (layer, be[i], 0, 0)),
            pl.BlockSpec((None, None, 1, D), lambda i, be, nu: (layer, be[i], 0, 0)),
        ],
        out_specs=pl.BlockSpec((MOE_BM, D), lambda i, be, nu: (i, 0)),
        scratch_shapes=[pltpu.VMEM((D, F2), BF16), pltpu.VMEM((F, D), BF16)],
    )
    L = w_gu.shape[0]
    return pl.pallas_call(
        _expert_kernel,
        grid_spec=grid_spec,
        out_shape=jax.ShapeDtypeStruct((n_rows, D), F32),
        compiler_params=_params(("arbitrary",)),
        name="expert",
    )(block_e, n_used, xs, w_gu, b_gu.reshape(L, E, 1, F2), w_down, b_down.reshape(L, E, 1, D))


def _combine_kernel(seg_ref, seg_next_ref, lpt_ref, gt_ref, ys_ref, lat_ref, m_ref, out_ref, ysort_ref, sems):
    i = pl.program_id(0)
    slot = i % 2

    def fetch(seg, s):
        def recv(local, src, rows):
            _chunk_copy(ys_ref, src, ysort_ref.at[s], local, sems.at[s], rows).start()

        _for_each_chunk(seg, recv)

    @pl.when(i == 0)
    def _():
        ysort_ref[...] = jnp.zeros_like(ysort_ref)
        fetch(seg_ref, 0)

    @pl.when(i + 1 < pl.num_programs(0))
    def _():
        fetch(seg_next_ref, 1 - slot)

    def count(e, totals):
        n_big, n_small = _chunk_counts(seg_ref, e)
        return totals[0] + n_big, totals[1] + n_small

    n_big, n_small = lax.fori_loop(0, N_EXPERTS, count, (0, 0))
    _wait_chunks(ys_ref, ysort_ref.at[slot], sems.at[slot], n_big, n_small)

    lpt = lpt_ref[...]
    gt = gt_ref[...]
    j_iota = lax.broadcasted_iota(jnp.int32, (lpt.shape[0], SORT_ROWS), 1)
    weights = jnp.zeros(j_iota.shape, F32)
    for k in range(TOP_K):
        weights = weights + jnp.where(lpt[:, k:k + 1] == j_iota, gt[:, k:k + 1], 0.0)
    w_hi = weights.astype(BF16)
    w_lo = (weights - w_hi.astype(F32)).astype(BF16)
    y = ysort_ref[slot]
    y_hi = y.astype(BF16)
    y_lo = (y - y_hi.astype(F32)).astype(BF16)
    f = (jnp.dot(w_hi, y_hi, preferred_element_type=F32) + jnp.dot(w_lo, y_hi, preferred_element_type=F32)
         + jnp.dot(w_hi, y_lo, preferred_element_type=F32))
    out_ref[...] = lat_ref[...] + m_ref[0][5:6] * f


def _combine_call(seg3, lpt, ys, gates_t, lat, mods, nt_lat, n_out):
    D = lat.shape[1]
    nt = seg3.shape[0]
    row = lambda i: (i, 0)
    return pl.pallas_call(
        _combine_kernel,
        grid=(n_out // TILE,),
        in_specs=[
            pl.BlockSpec((1, 1, LANE), lambda i: (i, 0, 0), memory_space=pltpu.SMEM),
            pl.BlockSpec((1, 1, LANE), lambda i: (jnp.minimum(i + 1, nt - 1), 0, 0), memory_space=pltpu.SMEM),
            pl.BlockSpec((TILE, LANE), row),
            pl.BlockSpec((TILE, LANE), row),
            pl.BlockSpec(memory_space=pl.ANY),
            pl.BlockSpec((TILE, D), row),
            pl.BlockSpec((1, 6, D), lambda i: (i // nt_lat, 0, 0)),
        ],
        out_specs=pl.BlockSpec((TILE, D), row),
        out_shape=jax.ShapeDtypeStruct((n_out, D), F32),
        scratch_shapes=[pltpu.VMEM((2, SORT_ROWS, D), F32), pltpu.SemaphoreType.DMA((2,))],
        compiler_params=_params(("arbitrary",)),
        name="combine",
    )(seg3, seg3, lpt, gates_t, ys, lat, mods)


def _block_diag(blocks):
    H, a, b = blocks.shape
    eye = jnp.eye(H, dtype=blocks.dtype)
    return (eye[:, None, :, None] * blocks[:, :, None, :]).reshape(H * a, H * b)


def _head_slabs(w, width):
    lead = w.shape[:-1]
    w = w.reshape(lead + (MLA_HEADS, width))
    w = jnp.pad(w, [(0, 0)] * len(lead) + [(0, 0), (0, HEAD_PAD - width)])
    return w.reshape(lead + (MLA_HEADS * HEAD_PAD,))


def _rope_tables(rows, n_ctx):
    n_freq = MLA_ROPE // 4
    inv = ROPE_BASE ** (-jnp.arange(n_freq, dtype=F32) / n_freq)
    r = jnp.repeat(jnp.arange(rows, dtype=F32), GRID_W)
    col = jnp.tile(jnp.arange(GRID_W, dtype=F32), rows)
    ang = jnp.concatenate([r[:, None] * inv, col[:, None] * inv], axis=-1)
    cos, sin = jnp.cos(ang), jnp.sin(ang)
    n = cos.shape[0]
    half = MLA_ROPE // 2
    pad = HEAD_PAD - MLA_QK
    c_tab = jnp.concatenate([jnp.ones((n, MLA_NOPE), F32), cos, cos, jnp.ones((n, pad), F32)], axis=1)
    s_tab = jnp.concatenate([jnp.zeros((n, MLA_NOPE), F32), -sin, sin, jnp.zeros((n, pad), F32)], axis=1)
    c_tab = jnp.concatenate([c_tab, jnp.ones((n_ctx, HEAD_PAD), F32)], axis=0)
    s_tab = jnp.concatenate([s_tab, jnp.zeros((n_ctx, HEAD_PAD), F32)], axis=0)
    del half
    return c_tab, s_tab


def kernel(x, c, ctx, c_ctx, w_mod, b_mod, norm_mix, norm_ffn, w_in, w_out, pool_w, pool_scale, mla_q_a_norm, mla_w_uq, mla_kv_a_norm, mla_w_ukv, mla_q_norm, mla_k_norm, sgu_norm_g, sgu_norm_b, sgu_ws, sgu_b, lru_conv_w, lru_conv_b, lru_wa, lru_ba, lru_wx, lru_bx, lru_lambda, router_w, router_b, moe_w_gu, moe_b_gu, moe_w_down, moe_b_down):
    B, N, D = x.shape
    n_ctx = ctx.shape[1]
    L = w_mod.shape[0]
    T = N + n_ctx
    assert B == 1 and N % TILE == 0 and n_ctx == TILE and T % TOK_TILE == 0 and TOK_TILE % SGU_CHUNK == 0
    nt = T // TILE
    nt_lat = N // TILE

    lat = jnp.concatenate([x[0], ctx[0]], axis=0)
    crows = jnp.zeros((8, D), F32).at[0].set(c[0]).at[1].set(c_ctx)
    mods_all = _mod_call(crows, w_mod, b_mod)
    c_tab, s_tab = _rope_tables(N // GRID_W, n_ctx)
    tri = jnp.triu(jnp.ones((TILE, TILE), F32), k=1).astype(BF16)
    ltri = jnp.tril(jnp.ones((N_EXPERTS, N_EXPERTS), F32), k=-1).astype(BF16)

    block_u = MOE_BM // SEG
    n_blocks = -(-(T * TOP_K + nt * N_EXPERTS * (SEG - 1) + N_EXPERTS * (MOE_BM - 1)) // MOE_BM)
    n_rows = n_blocks * MOE_BM

    for l in range(L):
        mods = mods_all[l, :2].reshape(2, 6, D)

        wa_, wqa, wkva, wkr, wc_, wdx, wdg = jnp.split(
            w_in[l], [256, 448, 576, 608, 1120, 1376], axis=1)
        wkr_placed = jnp.pad(wkr.reshape(D, 1, MLA_ROPE),
                             ((0, 0), (0, 0), (MLA_NOPE, HEAD_PAD - MLA_QK)))
        wkr_placed = jnp.tile(wkr_placed, (1, MLA_HEADS, 1)).reshape(D, MLA_HEADS * HEAD_PAD)
        win = jnp.concatenate(
            [wa_, jnp.pad(wqa, ((0, 0), (0, 256 - MLA_Q_RANK))), wkva, wkr_placed, wc_, wdx, wdg],
            axis=1).astype(BF16)
        assert win.shape[1] == IN_COLS_P
        gqa = jnp.pad(mla_q_a_norm[l], (0, 256 - MLA_Q_RANK)).reshape(1, 256)
        wuq = jnp.pad(_head_slabs(mla_w_uq[l], MLA_QK), ((0, 256 - MLA_Q_RANK), (0, 0))).astype(BF16)
        gkva = mla_kv_a_norm[l].reshape(1, MLA_KV_RANK)
        wukv3 = mla_w_ukv[l].reshape(MLA_KV_RANK, MLA_HEADS, MLA_NOPE + MLA_V)
        wuk = _head_slabs(wukv3[:, :, :MLA_NOPE].reshape(MLA_KV_RANK, -1), MLA_NOPE)
        wuv = _head_slabs(wukv3[:, :, MLA_NOPE:].reshape(MLA_KV_RANK, -1), MLA_V)
        wukv = jnp.concatenate([wuk, wuv], axis=1).astype(BF16)
        gq = jnp.pad(mla_q_norm[l], (0, HEAD_PAD - MLA_QK)).reshape(1, HEAD_PAD)
        gk = jnp.pad(mla_k_norm[l], (0, HEAD_PAD - MLA_QK)).reshape(1, HEAD_PAD)
        wscat = sgu_ws[l].transpose(1, 0, 2).reshape(SGU_CHUNK, SGU_HEADS * SGU_CHUNK).astype(BF16)
        sbias = jnp.repeat(sgu_b[l].T, SGU_HD, axis=1)

        pa, q, k, vt, c_s, xd, gd = _inproj_call(
            lat, mods, norm_mix[l].reshape(1, D), win, gqa, wuq, gkva, wukv, gq, gk, c_tab, s_tab,
            sgu_norm_g[l].reshape(1, GROUP_W), sgu_norm_b[l].reshape(1, GROUP_W), wscat, sbias, N)

        wpool = _block_diag(pool_w[l]).astype(BF16)
        wlru = jnp.concatenate(
            [_block_diag(lru_wa[l, 0]), _block_diag(lru_wx[l, 0]),
             _block_diag(lru_wa[l, 1]), _block_diag(lru_wx[l, 1])], axis=1).astype(BF16)
        blru = jnp.concatenate([lru_ba[l, 0], lru_bx[l, 0], lru_ba[l, 1], lru_bx[l, 1]]).reshape(1, -1)
        a_s, hf, ab, bb = _seq_fwd_call(
            pa, xd, wpool, pool_scale[l].reshape(1, GROUP_W), lru_conv_w[l],
            lru_conv_b[l].reshape(1, GROUP_W), wlru, blru, lru_lambda[l], nt_lat, N, n_ctx)
        d_s = _seq_bwd_call(ab, bb, hf, gd)

        tk = next(t for t in (3328, 1280, TILE) if T % t == 0)
        b_lat = _attn_call(q, k, vt, N, 512, 0, T, tk, 0)
        if l < L - 1:
            b_ctx = _attn_call(q, k, vt, n_ctx, n_ctx, N // n_ctx, n_ctx, n_ctx, N // n_ctx)
        else:
            b_ctx = jnp.zeros((n_ctx, MLA_HEADS * MLA_V), BF16)
        b_s = jnp.concatenate([b_lat, b_ctx], axis=0)

        rw_hi = router_w[l].astype(BF16)
        rw_lo = (router_w[l] - rw_hi.astype(F32)).astype(BF16)
        lane_pad = ((0, 0), (0, LANE - N_EXPERTS))
        rw = jnp.concatenate([jnp.pad(rw_hi, lane_pad), jnp.pad(rw_lo, lane_pad)], axis=1)
        lat2, fin, top_e, gates_t = _outproj_call(
            a_s, b_s, c_s, d_s, lat, mods, w_out[l].astype(BF16), norm_ffn[l].reshape(1, D),
            rw, router_b[l].reshape(N_EXPERTS, 1), N)
        lp, lpt, seg, tot = _plan_call(top_e, tri, ltri)

        seg = seg.reshape(nt, N_EXPERTS, LANE)
        tot_u = tot[:, 0].astype(jnp.int32)
        padded_u = (tot_u + block_u - 1) // block_u * block_u
        pad_end_u = jnp.cumsum(padded_u)
        pad_start_u = pad_end_u - padded_u
        seg3 = jnp.concatenate(
            [seg[:, :, 0], seg[:, :, 1], seg[:, :, 2] + pad_start_u[None, :],
             jnp.zeros((nt, LANE - 3 * N_EXPERTS), jnp.int32)], axis=1).reshape(nt, 1, LANE)
        block_u0 = jnp.arange(n_blocks, dtype=jnp.int32) * block_u
        block_e = jnp.minimum(
            jnp.sum((pad_end_u[None, :] <= block_u0[:, None]).astype(jnp.int32), axis=1),
            N_EXPERTS - 1)
        n_used = (pad_end_u[-1:] // block_u).astype(jnp.int32)

        xs = _dispatch_call(pad_start_u + tot_u, pad_end_u, seg3, lp, fin, n_rows)
        ys = _expert_call(block_e, n_used, xs, moe_w_gu, moe_b_gu, moe_w_down, moe_b_down, l)
        lat = _combine_call(seg3, lpt, ys, gates_t, lat2, mods, nt_lat, T)

    return lat[:N].reshape(B, N, D)
```

```python
import functools

import jax
import jax.numpy as jnp
from jax import lax
from jax.experimental import pallas as pl
from jax.experimental.pallas import tpu as pltpu

F32 = jnp.float32
BF16 = jnp.bfloat16
HIGHEST = lax.Precision.HIGHEST

EPS = 1e-6
LOG2_E = 1.4426950408889634
GRID_W = 64
GROUP_W = 256
POOL_WINDOWS = (2, 4, 8, 16)
POOL_CH = GROUP_W // len(POOL_WINDOWS)
MLA_HEADS = 4
MLA_NOPE = 64
MLA_ROPE = 32
MLA_QK = MLA_NOPE + MLA_ROPE
MLA_V = 64
MLA_Q_RANK = 192
MLA_KV_RANK = 128
ROPE_BASE = 10000.0
SGU_HEADS = 4
SGU_HD = GROUP_W // SGU_HEADS
SGU_CHUNK = 128
LRU_HEADS = 4
LRU_HD = GROUP_W // LRU_HEADS
CONV_W = 4
CONV_LEFT = 1
LRU_C = 8.0
N_EXPERTS = 32
TOP_K = 4
SWIGLU_LIMIT = 7.0
SWIGLU_ALPHA = 1.702

LANE = 128
HEAD_PAD = LANE
TILE = 256
TOK_TILE = 640
HALO = 16
V_ROWS = 80
ATTN_CHUNK = 256
MOE_BM = 512
VMEM_LIMIT = 56 * 1024 * 1024

C_A = 0
C_QA = C_A + GROUP_W
C_KVA = C_QA + 256
C_KR = C_KVA + MLA_KV_RANK
C_CU = C_KR + MLA_HEADS * HEAD_PAD
C_CV = C_CU + GROUP_W
C_DX = C_CV + GROUP_W
C_DG = C_DX + GROUP_W
IN_COLS_P = C_DG + GROUP_W


def _params(sem, vmem=VMEM_LIMIT):
    return pltpu.CompilerParams(dimension_semantics=sem, vmem_limit_bytes=vmem)


def _rms(x, n):
    return x * lax.rsqrt(jnp.sum(x * x, axis=-1, keepdims=True) * (1.0 / n) + EPS)


def _mod_kernel(c_ref, w_ref, b_ref, o_ref):
    cv = c_ref[...]
    s = cv * jax.nn.sigmoid(cv)
    o_ref[0] = jnp.dot(s, w_ref[0], precision=HIGHEST, preferred_element_type=F32) + b_ref[0]


def _mod_call(crows, w_mod, b_mod):
    L, D, M = w_mod.shape
    bn = 1536
    return pl.pallas_call(
        _mod_kernel,
        grid=(L, M // bn),
        in_specs=[
            pl.BlockSpec((8, D), lambda l, j: (0, 0)),
            pl.BlockSpec((1, D, bn), lambda l, j: (l, 0, j)),
            pl.BlockSpec((1, 1, bn), lambda l, j: (l, 0, j)),
        ],
        out_specs=pl.BlockSpec((1, 8, bn), lambda l, j: (l, 0, j)),
        out_shape=jax.ShapeDtypeStruct((L, 8, M), F32),
        compiler_params=_params(("arbitrary", "arbitrary")),
        name="mod",
    )(crows, w_mod, b_mod.reshape(L, 1, M))


def _rope_heads(x, gain, c_tab, s_tab, lane, scale):
    outs = []
    for h in range(MLA_HEADS):
        xh = x[:, h * HEAD_PAD:(h + 1) * HEAD_PAD]
        xn = _rms(xh, MLA_QK) * gain
        swap = jnp.where(lane < MLA_NOPE + MLA_ROPE // 2,
                         pltpu.roll(xn, HEAD_PAD - MLA_ROPE // 2, axis=1),
                         pltpu.roll(xn, MLA_ROPE // 2, axis=1))
        r = xn * c_tab + swap * s_tab
        if scale != 1.0:
            r = r * scale
        outs.append(r.astype(BF16))
    return jnp.concatenate(outs, axis=1)


def _mod_row(m_ref, is_ctx, k):
    return jnp.where(is_ctx, m_ref[1, k:k + 1, :], m_ref[0, k:k + 1, :])


def _is_ctx_rows(tm, n_lat):
    row = pl.program_id(0) * tm + lax.broadcasted_iota(jnp.int32, (tm, 1), 0)
    return row >= n_lat


def _inproj_kernel(n_lat, x_ref, m_ref, gn_ref, win_ref, gqa_ref, wuq_ref, gkva_ref, wukv_ref,
                   gq_ref, gk_ref, ct_ref, st_ref, sg_ref, sb_ref, ws_ref, sbias_ref,
                   pa_ref, q_ref, k_ref, vt_ref, c_ref, xd_ref, gd_ref):
    x = x_ref[...]
    is_ctx = _is_ctx_rows(x.shape[0], n_lat)
    h = _rms(x, x.shape[-1]) * gn_ref[...]
    h = h * (1.0 + _mod_row(m_ref, is_ctx, 1)) + _mod_row(m_ref, is_ctx, 0)
    p = jnp.dot(h.astype(BF16), win_ref[...], preferred_element_type=F32)

    pa_ref[...] = p[:, C_A:C_A + GROUP_W]
    xd_ref[...] = p[:, C_DX:C_DX + GROUP_W]
    gd_ref[...] = jax.nn.gelu(p[:, C_DG:C_DG + GROUP_W])

    lane = lax.broadcasted_iota(jnp.int32, (x.shape[0], HEAD_PAD), 1)
    c_tab = ct_ref[...]
    s_tab = st_ref[...]
    qa = _rms(p[:, C_QA:C_QA + 256], MLA_Q_RANK) * gqa_ref[...]
    q = jnp.dot(qa.astype(BF16), wuq_ref[...], preferred_element_type=F32)
    q_ref[...] = _rope_heads(q, gq_ref[...], c_tab, s_tab, lane, MLA_QK ** -0.5 * LOG2_E)

    kva = _rms(p[:, C_KVA:C_KVA + MLA_KV_RANK], MLA_KV_RANK) * gkva_ref[...]
    kv = jnp.dot(kva.astype(BF16), wukv_ref[...], preferred_element_type=F32)
    kpre = kv[:, :MLA_HEADS * HEAD_PAD] + p[:, C_KR:C_KR + MLA_HEADS * HEAD_PAD]
    k_ref[...] = _rope_heads(kpre, gk_ref[...], c_tab, s_tab, lane, 1.0)
    v = kv[:, MLA_HEADS * HEAD_PAD:]
    slab_lane = lax.broadcasted_iota(jnp.int32, v.shape, 1) % HEAD_PAD
    vt_ref[...] = jnp.where(slab_lane == MLA_V, 1.0, v).T.astype(BF16)

    z = jax.nn.gelu(p[:, C_CU:C_CU + 2 * GROUP_W])
    u = z[:, :GROUP_W]
    v = z[:, GROUP_W:]
    mu = jnp.mean(v, axis=-1, keepdims=True)
    vc = v - mu
    var = jnp.mean(vc * vc, axis=-1, keepdims=True)
    vn = (vc * lax.rsqrt(var + EPS) * sg_ref[...] + sb_ref[...]).astype(BF16)
    head_of_lane = lax.broadcasted_iota(jnp.int32, (SGU_CHUNK, GROUP_W), 1) // SGU_HD
    zero = jnp.zeros((SGU_CHUNK, GROUP_W), BF16)
    for cch in range(x.shape[0] // SGU_CHUNK):
        rows = slice(cch * SGU_CHUNK, (cch + 1) * SGU_CHUNK)
        vch = vn[rows]
        stacked = jnp.concatenate(
            [jnp.where(head_of_lane == hh, vch, zero) for hh in range(SGU_HEADS)], axis=0)
        mixed = jnp.dot(ws_ref[...], stacked, preferred_element_type=F32) + sbias_ref[...]
        c_ref[rows, :] = (u[rows] * mixed).astype(BF16)


def _inproj_call(xcat, mods, gn, win, gqa, wuq, gkva, wukv, gq, gk, ctab, stab, sg, sb, wscat, sbias,
                 n_lat):
    T, D = xcat.shape
    tm = TILE
    nt = T // tm
    row = lambda i: (i, 0)

    def full(a):
        return pl.BlockSpec(a.shape, lambda i: (0,) * a.ndim)

    out_shapes = (
        jax.ShapeDtypeStruct((T, GROUP_W), F32),
        jax.ShapeDtypeStruct((T, MLA_HEADS * HEAD_PAD), BF16),
        jax.ShapeDtypeStruct((T, MLA_HEADS * HEAD_PAD), BF16),
        jax.ShapeDtypeStruct((MLA_HEADS * HEAD_PAD, T), BF16),
        jax.ShapeDtypeStruct((T, GROUP_W), BF16),
        jax.ShapeDtypeStruct((T, GROUP_W), F32),
        jax.ShapeDtypeStruct((T, GROUP_W), F32),
    )
    out_specs = [pl.BlockSpec((tm, s.shape[1]), row) for s in out_shapes]
    out_specs[3] = pl.BlockSpec((MLA_HEADS * HEAD_PAD, tm), lambda i: (0, i))
    return pl.pallas_call(
        functools.partial(_inproj_kernel, n_lat),
        grid=(nt,),
        in_specs=[
            pl.BlockSpec((tm, D), row),
            full(mods),
            full(gn), full(win), full(gqa), full(wuq), full(gkva), full(wukv), full(gq), full(gk),
            pl.BlockSpec((tm, HEAD_PAD), row),
            pl.BlockSpec((tm, HEAD_PAD), row),
            full(sg), full(sb), full(wscat), full(sbias),
        ],
        out_specs=tuple(out_specs),
        out_shape=out_shapes,
        compiler_params=_params(("parallel",)),
        name="inproj",
    )(xcat, mods, gn, win, gqa, wuq, gkva, wukv, gq, gk, ctab, stab, sg, sb, wscat, sbias)


def _shift_rows(x, d, fill, reverse):
    n = x.shape[0]
    t = lax.broadcasted_iota(jnp.int32, x.shape, 0)
    if reverse:
        return jnp.where(t < n - d, pltpu.roll(x, n - d, axis=0), fill)
    return jnp.where(t >= d, pltpu.roll(x, d, axis=0), fill)


def _tile_scan(a, b, reverse):
    d = 1
    while d < a.shape[0]:
        a_s = _shift_rows(a, d, 1.0, reverse)
        b_s = _shift_rows(b, d, 0.0, reverse)
        b = a * b_s + b
        a = a * a_s
        d *= 2
    return a, b


def _lru_coeffs(xc, proj, sp, d):
    r = jax.nn.sigmoid(proj[:, (2 * d) * GROUP_W:(2 * d + 1) * GROUP_W])
    i = jax.nn.sigmoid(proj[:, (2 * d + 1) * GROUP_W:(2 * d + 2) * GROUP_W])
    log_a = -LRU_C * r * sp[d:d + 1]
    a = jnp.exp(log_a)
    drive = jnp.sqrt(1.0 - jnp.exp(2.0 * log_a)) * (i * xc)
    return a, drive


def _seq_fwd_kernel(nt_lat, n_lat, n_ctx,
                    pa_ref, pa_prev_ref, pa_next_ref, xd_ref, xd_prev_ref, xd_next_ref,
                    wpool_ref, pscale_ref, cw_ref, cb_ref, wlru_ref, blru_ref, lam_ref,
                    a_out_ref, hf_ref, ab_ref, bb_ref,
                    ext_ref, carry_ref):
    j = pl.program_id(0)
    nt = pl.num_programs(0)
    ti = (j + nt_lat) % nt
    is_ctx = ti >= nt_lat
    seq_first = jnp.logical_or(ti == 0, ti == nt_lat)
    seq_last = jnp.logical_or(ti == nt_lat - 1, ti == nt - 1)
    t_loc = (ti - jnp.where(is_ctx, nt_lat, 0)) * TILE
    n_seq = jnp.where(is_ctx, n_ctx, n_lat)

    @pl.when(j == 0)
    def _():
        carry_ref[...] = jnp.zeros_like(carry_ref)

    def load_ext(cur_ref, prev_ref, next_ref):
        ext_ref[0:HALO, :] = jnp.where(seq_first, 0.0, prev_ref[...])
        ext_ref[HALO:HALO + TILE, :] = cur_ref[...]
        ext_ref[HALO + TILE:, :] = jnp.where(seq_last, 0.0, next_ref[...])

    def win(off):
        return ext_ref[pl.ds(HALO + off, TILE), :]

    load_ext(pa_ref, pa_prev_ref, pa_next_ref)
    x = pa_ref[...]
    t = t_loc + lax.broadcasted_iota(jnp.int32, (TILE, GROUP_W), 0)
    lane = lax.broadcasted_iota(jnp.int32, (TILE, GROUP_W), 1)
    ext = ext_ref[...]
    n_ext = ext.shape[0]
    sums = {2: ext + pltpu.roll(ext, 1, axis=0)}
    w = 2
    while w < max(POOL_WINDOWS):
        prev = sums[w]
        sums[2 * w] = pltpu.roll(prev, w // 2, axis=0) + pltpu.roll(prev, n_ext - w // 2, axis=0)
        w *= 2
    mean = jnp.zeros((TILE, GROUP_W), F32)
    for g, w in enumerate(POOL_WINDOWS):
        acc = sums[w][HALO:HALO + TILE]
        cnt = jnp.minimum(t + w // 2, n_seq) - jnp.maximum(t - w // 2, 0)
        mean = jnp.where(lane // POOL_CH == g, acc / cnt.astype(F32), mean)
    diff = (mean - x).astype(BF16)
    pooled = jnp.dot(diff, wpool_ref[...], preferred_element_type=F32) * pscale_ref[...]
    a_out_ref[...] = pooled.astype(BF16)

    load_ext(xd_ref, xd_prev_ref, xd_next_ref)
    xc = jnp.zeros((TILE, GROUP_W), F32) + cb_ref[...]
    for k in range(CONV_W):
        xc = xc + win(k - CONV_LEFT) * cw_ref[k:k + 1, :]

    proj = jnp.dot(xc.astype(BF16), wlru_ref[...], preferred_element_type=F32) + blru_ref[...]
    lam = lam_ref[...]
    sp = jnp.maximum(-lam, 0.0) + jnp.log(1.0 + jnp.exp(-jnp.abs(lam)))
    a_f, b_f = _lru_coeffs(xc, proj, sp, 0)
    a_b, b_b = _lru_coeffs(xc, proj, sp, 1)
    ab_ref[...] = a_b
    bb_ref[...] = b_b

    big_a, big_b = _tile_scan(a_f, b_f, reverse=False)
    hf = big_b + big_a * carry_ref[0:1, :]
    hf_ref[...] = hf
    carry_ref[0:1, :] = hf[TILE - 1:TILE, :]


def _seq_fwd_call(pa, xd, wpool, pscale, cw, cb, wlru, blru, lam, nt_lat, n_lat, n_ctx):
    T = pa.shape[0]
    nt = T // TILE
    hb = TILE // HALO
    n_halo = T // HALO

    def tile_of(j):
        return (j + nt_lat) % nt

    cur = lambda j: (tile_of(j), 0)
    prev = lambda j: (jnp.maximum(tile_of(j) * hb - 1, 0), 0)
    nxt = lambda j: (jnp.minimum((tile_of(j) + 1) * hb, n_halo - 1), 0)
    fixed = lambda j: (0, 0)

    def full(a):
        return pl.BlockSpec(a.shape, fixed)

    tile_spec = pl.BlockSpec((TILE, GROUP_W), cur)
    out_shapes = (
        jax.ShapeDtypeStruct((T, GROUP_W), BF16),
        jax.ShapeDtypeStruct((T, GROUP_W), F32),
        jax.ShapeDtypeStruct((T, GROUP_W), F32),
        jax.ShapeDtypeStruct((T, GROUP_W), F32),
    )
    return pl.pallas_call(
        functools.partial(_seq_fwd_kernel, nt_lat, n_lat, n_ctx),
        grid=(nt,),
        in_specs=[
            tile_spec, pl.BlockSpec((HALO, GROUP_W), prev), pl.BlockSpec((HALO, GROUP_W), nxt),
            tile_spec, pl.BlockSpec((HALO, GROUP_W), prev), pl.BlockSpec((HALO, GROUP_W), nxt),
            full(wpool), full(pscale), full(cw), full(cb), full(wlru), full(blru), full(lam),
        ],
        out_specs=tuple(pl.BlockSpec((TILE, GROUP_W), cur) for _ in out_shapes),
        out_shape=out_shapes,
        scratch_shapes=[pltpu.VMEM((TILE + 2 * HALO, GROUP_W), F32), pltpu.VMEM((8, GROUP_W), F32)],
        compiler_params=_params(("arbitrary",)),
        name="seq_fwd",
    )(pa, pa, pa, xd, xd, xd, wpool, pscale, cw, cb, wlru, blru, lam)


def _seq_bwd_kernel(ab_ref, bb_ref, hf_ref, gd_ref, d_out_ref, carry_ref):
    @pl.when(pl.program_id(0) == 0)
    def _():
        carry_ref[...] = jnp.zeros_like(carry_ref)

    big_a, big_b = _tile_scan(ab_ref[...], bb_ref[...], reverse=True)
    hb = big_b + big_a * carry_ref[0:1, :]
    carry_ref[0:1, :] = hb[0:1, :]
    d_out_ref[...] = (gd_ref[...] * (hf_ref[...] + hb)).astype(BF16)


def _seq_bwd_call(ab, bb, hf, gd):
    T = ab.shape[0]
    nt = T // TILE
    spec = pl.BlockSpec((TILE, GROUP_W), lambda j: (nt - 1 - j, 0))
    return pl.pallas_call(
        _seq_bwd_kernel,
        grid=(nt,),
        in_specs=[spec, spec, spec, spec],
        out_specs=spec,
        out_shape=jax.ShapeDtypeStruct((T, GROUP_W), BF16),
        scratch_shapes=[pltpu.VMEM((8, GROUP_W), F32)],
        compiler_params=_params(("arbitrary",)),
        name="seq_bwd",
    )(ab, bb, hf, gd)


def _attn_kernel(q_ref, k_ref, vt_ref, o_ref, m_ref, acc_ref):
    kv = pl.program_id(1)

    @pl.when(kv == 0)
    def _():
        m_ref[...] = jnp.full_like(m_ref, -jnp.inf)
        acc_ref[...] = jnp.zeros_like(acc_ref)

    def head_cols(h):
        return slice(h * HEAD_PAD, (h + 1) * HEAD_PAD)

    ck = ATTN_CHUNK if k_ref.shape[0] % ATTN_CHUNK == 0 else k_ref.shape[0]
    items = [(c, h) for c in range(k_ref.shape[0] // ck) for h in range(MLA_HEADS)]

    def scores(c, h):
        cols = head_cols(h)
        return lax.dot_general(k_ref[c * ck:(c + 1) * ck, cols], q_ref[:, cols], (((1,), (1,)), ((), ())),
                               preferred_element_type=F32)

    def probs(h, s):
        m_prev = m_ref[h]
        m_new = jnp.maximum(m_prev, jnp.max(s, axis=0, keepdims=True))
        m_ref[h] = m_new
        return jnp.exp2(m_prev - m_new), jnp.exp2(s - m_new).astype(BF16)

    def accumulate(c, h, alpha, p):
        rows = slice(h * HEAD_PAD, h * HEAD_PAD + V_ROWS)
        acc_ref[h, :V_ROWS] = alpha * acc_ref[h, :V_ROWS] + jnp.dot(vt_ref[rows, c * ck:(c + 1) * ck], p,
                                                                    preferred_element_type=F32)

    s_next = scores(*items[0])
    pending = None
    for n, (c, h) in enumerate(items):
        s_cur = s_next
        if n + 1 < len(items):
            s_next = scores(*items[n + 1])
        if pending is not None:
            accumulate(*pending)
        pending = (c, h) + probs(h, s_cur)
    accumulate(*pending)

    @pl.when(kv == pl.num_programs(1) - 1)
    def _():
        outs = []
        for h in range(MLA_HEADS):
            a = acc_ref[h]
            outs.append((a / a[MLA_V:MLA_V + 1, :]).T[:, :MLA_V])
        o_ref[...] = jnp.concatenate(outs, axis=1).astype(BF16)


def _attn_call(q, k, vt, n_q, tq, q_blk0, n_k, tk, k_blk0):
    width = MLA_HEADS * HEAD_PAD
    return pl.pallas_call(
        _attn_kernel,
        grid=(n_q // tq, n_k // tk),
        in_specs=[
            pl.BlockSpec((tq, width), lambda i, j: (q_blk0 + i, 0)),
            pl.BlockSpec((tk, width), lambda i, j: (k_blk0 + j, 0)),
            pl.BlockSpec((width, tk), lambda i, j: (0, k_blk0 + j)),
        ],
        out_specs=pl.BlockSpec((tq, MLA_HEADS * MLA_V), lambda i, j: (i, 0)),
        out_shape=jax.ShapeDtypeStruct((n_q, MLA_HEADS * MLA_V), BF16),
        scratch_shapes=[
            pltpu.VMEM((MLA_HEADS, 1, tq), F32),
            pltpu.VMEM((MLA_HEADS, HEAD_PAD, tq), F32),
        ],
        compiler_params=_params(("parallel", "arbitrary")),
        name="attn",
    )(q, k, vt)


def _outproj_kernel(n_lat, a_ref, b_ref, c_ref, d_ref, x_ref, m_ref, wout_ref, gn_ref, rw_ref, rb_ref,
                    lat_ref, fin_ref, tope_ref, gatet_ref):
    tm = x_ref.shape[0]
    n_e = rb_ref.shape[0]
    is_ctx = _is_ctx_rows(tm, n_lat)
    y = jnp.dot(jnp.concatenate([a_ref[...], b_ref[...], c_ref[...], d_ref[...]], axis=1),
                wout_ref[...], preferred_element_type=F32)
    lat = x_ref[...] + _mod_row(m_ref, is_ctx, 2) * y
    lat_ref[...] = lat
    f = _rms(lat, lat.shape[-1]) * gn_ref[...]
    f = f * (1.0 + _mod_row(m_ref, is_ctx, 4)) + _mod_row(m_ref, is_ctx, 3)

    f_hi = f.astype(BF16)
    fin_ref[...] = f_hi
    f_lo = (f - f_hi.astype(F32)).astype(BF16)
    hi_prod = jnp.dot(f_hi, rw_ref[...], preferred_element_type=F32)
    lo_prod = jnp.dot(f_lo, rw_ref[:, :LANE], preferred_element_type=F32)
    logits_rows = hi_prod[:, :LANE] + hi_prod[:, LANE:] + lo_prod
    logits = logits_rows.T[:n_e] + rb_ref[...]
    e_iota = lax.broadcasted_iota(jnp.int32, (n_e, tm), 0)
    vals, idxs = [], []
    for _ in range(TOP_K):
        mx = jnp.max(logits, axis=0, keepdims=True)
        idx = jnp.min(jnp.where(logits == mx, e_iota, n_e), axis=0, keepdims=True)
        vals.append(mx)
        idxs.append(idx)
        logits = jnp.where(e_iota == idx, -jnp.inf, logits)
    ex = [jnp.exp(vv - vals[0]) for vv in vals]
    den = ex[0] + ex[1] + ex[2] + ex[3]
    row8 = lax.broadcasted_iota(jnp.int32, (8, tm), 0)
    row128 = lax.broadcasted_iota(jnp.int32, (LANE, tm), 0)
    tope = jnp.zeros((8, tm), jnp.int32)
    gates = jnp.zeros((LANE, tm), F32)
    for k in range(TOP_K):
        tope = jnp.where(row8 == k, idxs[k], tope)
        gates = jnp.where(row128 == k, ex[k] / den, gates)
    tope_ref[...] = tope
    gatet_ref[...] = gates.T


def _outproj_call(a, b, c, d, xcat, mods, wout, gn, rw, rb, n_lat):
    T, D = xcat.shape
    tm = TOK_TILE
    row = lambda i: (i, 0)

    def full(arr):
        return pl.BlockSpec(arr.shape, lambda i: (0,) * arr.ndim)

    g_spec = pl.BlockSpec((tm, GROUP_W), row)
    out_shapes = (
        jax.ShapeDtypeStruct((T, D), F32),
        jax.ShapeDtypeStruct((T, D), BF16),
        jax.ShapeDtypeStruct((8, T), jnp.int32),
        jax.ShapeDtypeStruct((T, LANE), F32),
    )
    out_specs = (
        pl.BlockSpec((tm, D), row), pl.BlockSpec((tm, D), row),
        pl.BlockSpec((8, tm), lambda i: (0, i)),
        pl.BlockSpec((tm, LANE), row),
    )
    return pl.pallas_call(
        functools.partial(_outproj_kernel, n_lat),
        grid=(T // tm,),
        in_specs=[
            g_spec, g_spec, g_spec, g_spec,
            pl.BlockSpec((tm, D), row),
            full(mods), full(wout), full(gn), full(rw), full(rb),
        ],
        out_specs=out_specs,
        out_shape=out_shapes,
        compiler_params=_params(("parallel",)),
        name="outproj",
    )(a, b, c, d, xcat, mods, wout, gn, rw, rb)


SEG = 8
BIG_UNITS = 4
BIG_ROWS = BIG_UNITS * SEG
SORT_ROWS = TOP_K * TILE + N_EXPERTS * SEG
SEG_START, SEG_COUNT, SEG_DEST = 0, N_EXPERTS, 2 * N_EXPERTS


def _plan_kernel(tope_ref, tri_ref, ltri_ref, lp_ref, lpt_ref, seg_ref, tot_ref, carry_ref):
    @pl.when(pl.program_id(0) == 0)
    def _():
        carry_ref[...] = jnp.zeros_like(carry_ref)

    tope = tope_ref[...]
    n_e = ltri_ref.shape[0]
    tm = tope.shape[1]
    e_iota = lax.broadcasted_iota(jnp.int32, (n_e, tm), 0)
    onehot = jnp.concatenate(
        [jnp.where(e_iota == tope[k:k + 1, :], 1.0, 0.0) for k in range(TOP_K)], axis=0)
    prefix = jnp.dot(onehot.astype(BF16), tri_ref[...], preferred_element_type=F32)
    cnt_k = [jnp.sum(onehot[k * n_e:(k + 1) * n_e], axis=1, keepdims=True) for k in range(TOP_K)]
    cnt = cnt_k[0] + cnt_k[1] + cnt_k[2] + cnt_k[3]
    units = jnp.floor((cnt + (SEG - 1)) * (1.0 / SEG))
    units_b = jnp.broadcast_to(units, (n_e, LANE))
    start_u = jnp.dot(ltri_ref[...], units_b.astype(BF16), preferred_element_type=F32)
    base = start_u[:, 0:1] * SEG
    row8 = lax.broadcasted_iota(jnp.int32, (8, tm), 0)
    row128 = lax.broadcasted_iota(jnp.int32, (LANE, tm), 0)
    lp8 = jnp.zeros((8, tm), F32)
    lp128 = jnp.zeros((LANE, tm), F32)
    for k in range(TOP_K):
        hot = onehot[k * n_e:(k + 1) * n_e]
        lp_k = jnp.sum(hot * (prefix[k * n_e:(k + 1) * n_e] + base), axis=0, keepdims=True)
        lp8 = jnp.where(row8 == k, lp_k, lp8)
        lp128 = jnp.where(row128 == k, lp_k, lp128)
        base = base + cnt_k[k]
    lp_ref[...] = lp8.astype(jnp.int32)
    lpt_ref[...] = lp128.T.astype(jnp.int32)
    lane = lax.broadcasted_iota(jnp.int32, (n_e, LANE), 1)
    carry = carry_ref[...]
    seg = jnp.where(lane == 0, start_u, jnp.where(lane == 1, units_b, jnp.where(lane == 2, carry, 0.0)))
    seg_ref[...] = seg.astype(jnp.int32)
    carry_ref[...] = carry + units_b
    tot_ref[...] = carry + units_b


def _plan_call(top_e, tri, ltri):
    T = top_e.shape[1]
    nt = T // TILE
    n_e = ltri.shape[0]
    fixed = lambda i: (0, 0)
    out_shapes = (
        jax.ShapeDtypeStruct((8, T), jnp.int32),
        jax.ShapeDtypeStruct((T, LANE), jnp.int32),
        jax.ShapeDtypeStruct((nt * n_e, LANE), jnp.int32),
        jax.ShapeDtypeStruct((n_e, LANE), F32),
    )
    return pl.pallas_call(
        _plan_kernel,
        grid=(nt,),
        in_specs=[pl.BlockSpec((8, TILE), lambda i: (0, i)), pl.BlockSpec(tri.shape, fixed),
                  pl.BlockSpec(ltri.shape, fixed)],
        out_specs=(pl.BlockSpec((8, TILE), lambda i: (0, i)), pl.BlockSpec((TILE, LANE), lambda i: (i, 0)),
                   pl.BlockSpec((n_e, LANE), lambda i: (i, 0)), pl.BlockSpec((n_e, LANE), fixed)),
        out_shape=out_shapes,
        scratch_shapes=[pltpu.VMEM((n_e, LANE), F32)],
        compiler_params=_params(("arbitrary",)),
        name="plan",
    )(top_e, tri, ltri)


def _chunk_copy(src, s_row, dst, d_row, sem, rows=SEG):
    return pltpu.make_async_copy(src.at[pl.ds(pl.multiple_of(s_row, SEG), rows)],
                                 dst.at[pl.ds(pl.multiple_of(d_row, SEG), rows)], sem)


def _chunk_counts(seg_ref, e):
    n = seg_ref[0, 0, SEG_COUNT + e]
    n_big = n // BIG_UNITS
    return n_big, n - n_big * BIG_UNITS


def _for_each_chunk(seg_ref, fn):
    def per_expert(e, totals):
        n_big, n_small = _chunk_counts(seg_ref, e)
        local = seg_ref[0, 0, SEG_START + e] * SEG
        dest = seg_ref[0, 0, SEG_DEST + e] * SEG

        def big(j, carry):
            fn(local + j * BIG_ROWS, dest + j * BIG_ROWS, BIG_ROWS)
            return carry

        def small(j, carry):
            off = n_big * BIG_ROWS + j * SEG
            fn(local + off, dest + off, SEG)
            return carry

        lax.fori_loop(0, n_big, big, 0)
        lax.fori_loop(0, n_small, small, 0)
        return totals[0] + n_big, totals[1] + n_small

    return lax.fori_loop(0, N_EXPERTS, per_expert, (0, 0))


def _wait_chunks(src, dst, sem, n_big, n_small):
    def wait_big(j, c):
        _chunk_copy(src, 0, dst, 0, sem, BIG_ROWS).wait()
        return c

    def wait_small(j, c):
        _chunk_copy(src, 0, dst, 0, sem).wait()
        return c

    lax.fori_loop(0, n_big, wait_big, 0)
    lax.fori_loop(0, n_small, wait_small, 0)


def _dispatch_kernel(lo_ref, hi_ref, seg_ref, lp_ref, fin_ref, xs_ref, sorted_ref, zero_ref, cnt_ref, sems, zero_sem,
                     free_sem):
    i = pl.program_id(0)
    slot = i % 2

    @pl.when(i == 0)
    def _():
        for n in range(4):
            cnt_ref[n] = 0
        zero_ref[...] = jnp.zeros_like(zero_ref)

        def per_expert(e, carry):
            def start(u, c):
                _chunk_copy(zero_ref, 0, xs_ref, u * SEG, zero_sem).start()
                return c

            def wait(u, c):
                _chunk_copy(zero_ref, 0, xs_ref, 0, zero_sem).wait()
                return c

            lax.fori_loop(lo_ref[e], hi_ref[e], start, 0)
            lax.fori_loop(lo_ref[e], hi_ref[e], wait, 0)
            return carry

        lax.fori_loop(0, N_EXPERTS, per_expert, 0)

    first_free = hi_ref[N_EXPERTS - 1] // (MOE_BM // SEG)
    n_blocks = xs_ref.shape[0] // MOE_BM

    def free_block_copy(b):
        return pltpu.make_async_copy(zero_ref, xs_ref.at[pl.ds(pl.multiple_of(b * MOE_BM, MOE_BM), MOE_BM)], free_sem)

    @pl.when(i == 0)
    def _():
        def start(b, c):
            free_block_copy(b).start()
            return c

        lax.fori_loop(first_free, n_blocks, start, 0)

    def drain(s):
        _wait_chunks(sorted_ref.at[s], xs_ref, sems.at[s], cnt_ref[2 * s], cnt_ref[2 * s + 1])

    drain(slot)
    lp = lp_ref[...]
    j_iota = lax.broadcasted_iota(jnp.int32, (SORT_ROWS, lp.shape[1]), 0)
    perm = jnp.zeros(j_iota.shape, F32)
    for k in range(TOP_K):
        perm = jnp.where(j_iota == lp[k:k + 1, :], 1.0, perm)
    sorted_ref[slot] = jnp.dot(perm.astype(BF16), fin_ref[...], preferred_element_type=F32)

    def send(local, dest, rows):
        _chunk_copy(sorted_ref.at[slot], local, xs_ref, dest, sems.at[slot], rows).start()

    n_big, n_small = _for_each_chunk(seg_ref, send)
    cnt_ref[2 * slot] = n_big
    cnt_ref[2 * slot + 1] = n_small

    @pl.when(i == pl.num_programs(0) - 1)
    def _():
        drain(slot)
        drain(1 - slot)

        def wait(b, c):
            free_block_copy(b).wait()
            return c

        lax.fori_loop(first_free, n_blocks, wait, 0)


def _dispatch_call(pad_lo, pad_hi, seg3, lp, fin, n_rows):
    nt = seg3.shape[0]
    D = fin.shape[1]
    grid_spec = pltpu.PrefetchScalarGridSpec(
        num_scalar_prefetch=2,
        grid=(nt,),
        in_specs=[
            pl.BlockSpec((1, 1, LANE), lambda i, lo, hi: (i, 0, 0), memory_space=pltpu.SMEM),
            pl.BlockSpec((8, TILE), lambda i, lo, hi: (0, i)),
            pl.BlockSpec((TILE, D), lambda i, lo, hi: (i, 0)),
        ],
        out_specs=pl.BlockSpec(memory_space=pl.ANY),
        scratch_shapes=[pltpu.VMEM((2, SORT_ROWS, D), F32), pltpu.VMEM((MOE_BM, D), F32), pltpu.SMEM((4,), jnp.int32),
                        pltpu.SemaphoreType.DMA((2,)), pltpu.SemaphoreType.DMA(()), pltpu.SemaphoreType.DMA(())],
    )
    return pl.pallas_call(
        _dispatch_kernel,
        grid_spec=grid_spec,
        out_shape=jax.ShapeDtypeStruct((n_rows, D), F32),
        compiler_params=_params(("arbitrary",)),
        name="dispatch",
    )(pad_lo, pad_hi, seg3, lp, fin)


def _expert_kernel(be_ref, nu_ref, xs_ref, wgu_ref, bgu_ref, wdn_ref, bdn_ref, ys_ref, wgu_bf, wdn_bf):
    i = pl.program_id(0)
    changed = jnp.logical_or(i == 0, be_ref[i] != be_ref[jnp.maximum(i - 1, 0)])

    @pl.when(changed)
    def _():
        wgu_bf[...] = wgu_ref[...].astype(BF16)
        wdn_bf[...] = wdn_ref[...].astype(BF16)

    @pl.when(i < nu_ref[0])
    def _():
        f = wdn_bf.shape[0]
        gu = jnp.dot(xs_ref[...].astype(BF16), wgu_bf[...], preferred_element_type=F32) + bgu_ref[...]
        g = jnp.minimum(gu[:, :f], SWIGLU_LIMIT)
        u = jnp.clip(gu[:, f:], -SWIGLU_LIMIT, SWIGLU_LIMIT)
        act = (u + 1.0) * (g * jax.nn.sigmoid(SWIGLU_ALPHA * g))
        ys_ref[...] = jnp.dot(act.astype(BF16), wdn_bf[...], preferred_element_type=F32) + bdn_ref[...]

    @pl.when(i >= nu_ref[0])
    def _():
        ys_ref[...] = jnp.zeros_like(ys_ref)


def _expert_call(block_e, n_used, xs, w_gu, b_gu, w_down, b_down, layer):
    n_rows, D = xs.shape
    _, E, _, F2 = w_gu.shape
    F = F2 // 2
    grid_spec = pltpu.PrefetchScalarGridSpec(
        num_scalar_prefetch=2,
        grid=(n_rows // MOE_BM,),
        in_specs=[
            pl.BlockSpec((MOE_BM, D), lambda i, be, nu: (jnp.minimum(i, nu[0] - 1), 0)),
            pl.BlockSpec((None, None, D, F2), lambda i, be, nu: (layer, be[i], 0, 0)),
            pl.BlockSpec((None, None, 1, F2), lambda i, be, nu: (layer, be[i], 0, 0)),
            pl.BlockSpec((None, None, F, D), lambda i, be, nu: (layer, be[i], 0, 0)),
            pl.BlockSpec((None, None, 1, D), lambda i, be, nu: (layer, be[i], 0, 0)),
        ],
        out_specs=pl.BlockSpec((MOE_BM, D), lambda i, be, nu: (i, 0)),
        scratch_shapes=[pltpu.VMEM((D, F2), BF16), pltpu.VMEM((F, D), BF16)],
    )
    L = w_gu.shape[0]
    return pl.pallas_call(
        _expert_kernel,
        grid_spec=grid_spec,
        out_shape=jax.ShapeDtypeStruct((n_rows, D), F32),
        compiler_params=_params(("arbitrary",)),
        name="expert",
    )(block_e, n_used, xs, w_gu, b_gu.reshape(L, E, 1, F2), w_down, b_down.reshape(L, E, 1, D))


def _combine_kernel(n_store, seg_ref, seg_next_ref, lpt_ref, gt_ref, ys_ref, lat_ref, m_ref, out_ref, ysort_ref,
                    sems):
    i = pl.program_id(0)
    slot = i % 2

    def fetch(seg, s):
        def recv(local, src, rows):
            _chunk_copy(ys_ref, src, ysort_ref.at[s], local, sems.at[s], rows).start()

        _for_each_chunk(seg, recv)

    @pl.when(i == 0)
    def _():
        ysort_ref[...] = jnp.zeros_like(ysort_ref)
        fetch(seg_ref, 0)

    @pl.when(i + 1 < pl.num_programs(0))
    def _():
        fetch(seg_next_ref, 1 - slot)

    def count(e, totals):
        n_big, n_small = _chunk_counts(seg_ref, e)
        return totals[0] + n_big, totals[1] + n_small

    n_big, n_small = lax.fori_loop(0, N_EXPERTS, count, (0, 0))
    _wait_chunks(ys_ref, ysort_ref.at[slot], sems.at[slot], n_big, n_small)

    @pl.when(i < n_store)
    def _():
        lpt = lpt_ref[...]
        gt = gt_ref[...]
        j_iota = lax.broadcasted_iota(jnp.int32, (lpt.shape[0], SORT_ROWS), 1)
        weights = jnp.zeros(j_iota.shape, F32)
        for k in range(TOP_K):
            weights = weights + jnp.where(lpt[:, k:k + 1] == j_iota, gt[:, k:k + 1], 0.0)
        w_hi = weights.astype(BF16)
        w_lo = (weights - w_hi.astype(F32)).astype(BF16)
        y = ysort_ref[slot]
        y_hi = y.astype(BF16)
        y_lo = (y - y_hi.astype(F32)).astype(BF16)
        f = (jnp.dot(w_hi, y_hi, preferred_element_type=F32) + jnp.dot(w_lo, y_hi, preferred_element_type=F32)
             + jnp.dot(w_hi, y_lo, preferred_element_type=F32))
        out_ref[...] = lat_ref[...] + m_ref[0][5:6] * f


def _combine_call(seg3, lpt, ys, gates_t, lat, mods, nt_lat, n_out):
    D = lat.shape[1]
    nt = seg3.shape[0]
    n_store = n_out // TILE
    row = lambda i: (i, 0)
    return pl.pallas_call(
        functools.partial(_combine_kernel, n_store),
        grid=(nt,),
        in_specs=[
            pl.BlockSpec((1, 1, LANE), lambda i: (i, 0, 0), memory_space=pltpu.SMEM),
            pl.BlockSpec((1, 1, LANE), lambda i: (jnp.minimum(i + 1, nt - 1), 0, 0), memory_space=pltpu.SMEM),
            pl.BlockSpec((TILE, LANE), row),
            pl.BlockSpec((TILE, LANE), row),
            pl.BlockSpec(memory_space=pl.ANY),
            pl.BlockSpec((TILE, D), row),
            pl.BlockSpec((1, 6, D), lambda i: (i // nt_lat, 0, 0)),
        ],
        out_specs=pl.BlockSpec((TILE, D), lambda i: (jnp.minimum(i, n_store - 1), 0)),
        out_shape=jax.ShapeDtypeStruct((n_out, D), F32),
        scratch_shapes=[pltpu.VMEM((2, SORT_ROWS, D), F32), pltpu.SemaphoreType.DMA((2,))],
        compiler_params=_params(("arbitrary",)),
        name="combine",
    )(seg3, seg3, lpt, gates_t, ys, lat, mods)


def _block_diag(blocks):
    H, a, b = blocks.shape
    eye = jnp.eye(H, dtype=blocks.dtype)
    return (eye[:, None, :, None] * blocks[:, :, None, :]).reshape(H * a, H * b)


def _head_slabs(w, width):
    lead = w.shape[:-1]
    w = w.reshape(lead + (MLA_HEADS, width))
    w = jnp.pad(w, [(0, 0)] * len(lead) + [(0, 0), (0, HEAD_PAD - width)])
    return w.reshape(lead + (MLA_HEADS * HEAD_PAD,))


def _rope_tables(rows, n_ctx):
    n_freq = MLA_ROPE // 4
    inv = ROPE_BASE ** (-jnp.arange(n_freq, dtype=F32) / n_freq)
    r = jnp.repeat(jnp.arange(rows, dtype=F32), GRID_W)
    col = jnp.tile(jnp.arange(GRID_W, dtype=F32), rows)
    ang = jnp.concatenate([r[:, None] * inv, col[:, None] * inv], axis=-1)
    cos, sin = jnp.cos(ang), jnp.sin(ang)
    n = cos.shape[0]
    half = MLA_ROPE // 2
    pad = HEAD_PAD - MLA_QK
    c_tab = jnp.concatenate([jnp.ones((n, MLA_NOPE), F32), cos, cos, jnp.ones((n, pad), F32)], axis=1)
    s_tab = jnp.concatenate([jnp.zeros((n, MLA_NOPE), F32), -sin, sin, jnp.zeros((n, pad), F32)], axis=1)
    c_tab = jnp.concatenate([c_tab, jnp.ones((n_ctx, HEAD_PAD), F32)], axis=0)
    s_tab = jnp.concatenate([s_tab, jnp.zeros((n_ctx, HEAD_PAD), F32)], axis=0)
    del half
    return c_tab, s_tab


def kernel(x, c, ctx, c_ctx, w_mod, b_mod, norm_mix, norm_ffn, w_in, w_out, pool_w, pool_scale, mla_q_a_norm, mla_w_uq, mla_kv_a_norm, mla_w_ukv, mla_q_norm, mla_k_norm, sgu_norm_g, sgu_norm_b, sgu_ws, sgu_b, lru_conv_w, lru_conv_b, lru_wa, lru_ba, lru_wx, lru_bx, lru_lambda, router_w, router_b, moe_w_gu, moe_b_gu, moe_w_down, moe_b_down):
    B, N, D = x.shape
    n_ctx = ctx.shape[1]
    L = w_mod.shape[0]
    T = N + n_ctx
    assert B == 1 and N % TILE == 0 and n_ctx == TILE and T % TOK_TILE == 0 and TOK_TILE % SGU_CHUNK == 0
    nt = T // TILE
    nt_lat = N // TILE

    lat = jnp.concatenate([x[0], ctx[0]], axis=0)
    crows = jnp.zeros((8, D), F32).at[0].set(c[0]).at[1].set(c_ctx)
    mods_all = _mod_call(crows, w_mod, b_mod)
    c_tab, s_tab = _rope_tables(N // GRID_W, n_ctx)
    tri = jnp.triu(jnp.ones((TILE, TILE), F32), k=1).astype(BF16)
    ltri = jnp.tril(jnp.ones((N_EXPERTS, N_EXPERTS), F32), k=-1).astype(BF16)

    block_u = MOE_BM // SEG
    n_blocks = -(-(T * TOP_K + nt * N_EXPERTS * (SEG - 1) + N_EXPERTS * (MOE_BM - 1)) // MOE_BM)
    n_rows = n_blocks * MOE_BM

    for l in range(L):
        mods = mods_all[l, :2].reshape(2, 6, D)

        wa_, wqa, wkva, wkr, wc_, wdx, wdg = jnp.split(
            w_in[l], [256, 448, 576, 608, 1120, 1376], axis=1)
        wkr_placed = jnp.pad(wkr.reshape(D, 1, MLA_ROPE),
                             ((0, 0), (0, 0), (MLA_NOPE, HEAD_PAD - MLA_QK)))
        wkr_placed = jnp.tile(wkr_placed, (1, MLA_HEADS, 1)).reshape(D, MLA_HEADS * HEAD_PAD)
        win = jnp.concatenate(
            [wa_, jnp.pad(wqa, ((0, 0), (0, 256 - MLA_Q_RANK))), wkva, wkr_placed, wc_, wdx, wdg],
            axis=1).astype(BF16)
        assert win.shape[1] == IN_COLS_P
        gqa = jnp.pad(mla_q_a_norm[l], (0, 256 - MLA_Q_RANK)).reshape(1, 256)
        wuq = jnp.pad(_head_slabs(mla_w_uq[l], MLA_QK), ((0, 256 - MLA_Q_RANK), (0, 0))).astype(BF16)
        gkva = mla_kv_a_norm[l].reshape(1, MLA_KV_RANK)
        wukv3 = mla_w_ukv[l].reshape(MLA_KV_RANK, MLA_HEADS, MLA_NOPE + MLA_V)
        wuk = _head_slabs(wukv3[:, :, :MLA_NOPE].reshape(MLA_KV_RANK, -1), MLA_NOPE)
        wuv = _head_slabs(wukv3[:, :, MLA_NOPE:].reshape(MLA_KV_RANK, -1), MLA_V)
        wukv = jnp.concatenate([wuk, wuv], axis=1).astype(BF16)
        gq = jnp.pad(mla_q_norm[l], (0, HEAD_PAD - MLA_QK)).reshape(1, HEAD_PAD)
        gk = jnp.pad(mla_k_norm[l], (0, HEAD_PAD - MLA_QK)).reshape(1, HEAD_PAD)
        wscat = sgu_ws[l].transpose(1, 0, 2).reshape(SGU_CHUNK, SGU_HEADS * SGU_CHUNK).astype(BF16)
        sbias = jnp.repeat(sgu_b[l].T, SGU_HD, axis=1)

        pa, q, k, vt, c_s, xd, gd = _inproj_call(
            lat, mods, norm_mix[l].reshape(1, D), win, gqa, wuq, gkva, wukv, gq, gk, c_tab, s_tab,
            sgu_norm_g[l].reshape(1, GROUP_W), sgu_norm_b[l].reshape(1, GROUP_W), wscat, sbias, N)

        wpool = _block_diag(pool_w[l]).astype(BF16)
        wlru = jnp.concatenate(
            [_block_diag(lru_wa[l, 0]), _block_diag(lru_wx[l, 0]),
             _block_diag(lru_wa[l, 1]), _block_diag(lru_wx[l, 1])], axis=1).astype(BF16)
        blru = jnp.concatenate([lru_ba[l, 0], lru_bx[l, 0], lru_ba[l, 1], lru_bx[l, 1]]).reshape(1, -1)
        a_s, hf, ab, bb = _seq_fwd_call(
            pa, xd, wpool, pool_scale[l].reshape(1, GROUP_W), lru_conv_w[l],
            lru_conv_b[l].reshape(1, GROUP_W), wlru, blru, lru_lambda[l], nt_lat, N, n_ctx)
        d_s = _seq_bwd_call(ab, bb, hf, gd)

        tk = next(t for t in (3328, 1280, TILE) if T % t == 0)
        b_lat = _attn_call(q, k, vt, N, 512, 0, T, tk, 0)
        if l < L - 1:
            b_ctx = _attn_call(q, k, vt, n_ctx, n_ctx, N // n_ctx, n_ctx, n_ctx, N // n_ctx)
        else:
            b_ctx = jnp.zeros((n_ctx, MLA_HEADS * MLA_V), BF16)
        b_s = jnp.concatenate([b_lat, b_ctx], axis=0)

        rw_hi = router_w[l].astype(BF16)
        rw_lo = (router_w[l] - rw_hi.astype(F32)).astype(BF16)
        lane_pad = ((0, 0), (0, LANE - N_EXPERTS))
        rw = jnp.concatenate([jnp.pad(rw_hi, lane_pad), jnp.pad(rw_lo, lane_pad)], axis=1)
        lat2, fin, top_e, gates_t = _outproj_call(
            a_s, b_s, c_s, d_s, lat, mods, w_out[l].astype(BF16), norm_ffn[l].reshape(1, D),
            rw, router_b[l].reshape(N_EXPERTS, 1), N)
        lp, lpt, seg, tot = _plan_call(top_e, tri, ltri)

        seg = seg.reshape(nt, N_EXPERTS, LANE)
        tot_u = tot[:, 0].astype(jnp.int32)
        padded_u = (tot_u + block_u - 1) // block_u * block_u
        pad_end_u = jnp.cumsum(padded_u)
        pad_start_u = pad_end_u - padded_u
        seg3 = jnp.concatenate(
            [seg[:, :, 0], seg[:, :, 1], seg[:, :, 2] + pad_start_u[None, :],
             jnp.zeros((nt, LANE - 3 * N_EXPERTS), jnp.int32)], axis=1).reshape(nt, 1, LANE)
        block_u0 = jnp.arange(n_blocks, dtype=jnp.int32) * block_u
        block_e = jnp.minimum(
            jnp.sum((pad_end_u[None, :] <= block_u0[:, None]).astype(jnp.int32), axis=1),
            N_EXPERTS - 1)
        n_used = (pad_end_u[-1:] // block_u).astype(jnp.int32)

        xs = _dispatch_call(pad_start_u + tot_u, pad_end_u, seg3, lp, fin, n_rows)
        ys = _expert_call(block_e, n_used, xs, moe_w_gu, moe_b_gu, moe_w_down, moe_b_down, l)
        lat = _combine_call(seg3, lpt, ys, gates_t, lat2, mods, nt_lat, T if l < L - 1 else N)

    return lat.reshape(B, N, D)
```
